```python
import math
import jax, jax.numpy as jnp
from jax import lax
import numpy as np

D_MODEL = 2048
BATCH = 4
SEQ = 4096
DEPTH = 4

CHUNK = 64
Q_BLOCK = 128
N_MIXERS = 4
EPS = 1e-6
ROPE_BASE = 10000.0

SB_HEADS = 16
SB_HEAD_DIM = D_MODEL // SB_HEADS
SB_WIDTH = SB_HEADS * SB_HEAD_DIM
SC_WIDTH = D_MODEL
CONV_WIDTH = 3
MLA_HEADS = 16
MLA_NOPE_DIM = 128
MLA_ROPE_DIM = 64
MLA_V_DIM = 128
MLA_Q_RANK = D_MODEL // 4
MLA_KV_RANK = D_MODEL // 8
MLA_WIDTH = MLA_HEADS * MLA_V_DIM
MLA_IN_WIDTH = MLA_Q_RANK + MLA_KV_RANK + MLA_ROPE_DIM + MLA_WIDTH
RET_HEADS = 8
RET_QK_DIM = D_MODEL // RET_HEADS
RET_V_DIM = 2 * RET_QK_DIM
RET_QK_WIDTH = RET_HEADS * RET_QK_DIM
RET_V_WIDTH = RET_HEADS * RET_V_DIM

kernel_name = 'hybrid_interleaved_streaming_trunk'


def n_layers_of(kind):
    return len(range(kind, DEPTH, N_MIXERS))


def rmsnorm(x, g):
    xf = x.astype(jnp.float32)
    y = xf * lax.rsqrt(jnp.mean(xf * xf, axis=-1, keepdims=True) + EPS)
    return (y * g.astype(jnp.float32)).astype(x.dtype)


def rope(x, positions):
    half = x.shape[-1] // 2
    inv_freq = jnp.exp(-math.log(ROPE_BASE) * jnp.arange(half, dtype=jnp.float32) / half)
    ang = positions.astype(jnp.float32)[:, :, None, None] * inv_freq
    cos, sin = jnp.cos(ang), jnp.sin(ang)
    xf = x.astype(jnp.float32)
    x1, x2 = xf[..., :half], xf[..., half:]
    return jnp.concatenate([x1 * cos - x2 * sin, x1 * sin + x2 * cos], axis=-1).astype(x.dtype)


def stick_breaking_mixer(xn, w_in, w_out):
    b, s, _ = xn.shape
    q, k, v, gate = jnp.split(xn @ w_in, 4, axis=-1)
    q = q.reshape(b, s, SB_HEADS, SB_HEAD_DIM)
    k = k.reshape(b, s, SB_HEADS, SB_HEAD_DIM)
    v = v.reshape(b, s, SB_HEADS, SB_HEAD_DIM)
    scale = 1.0 / math.sqrt(SB_HEAD_DIM)
    outs = []
    for blk in range(s // Q_BLOCK):
        lo, hi = blk * Q_BLOCK, (blk + 1) * Q_BLOCK
        z = jnp.einsum('bqhd,bkhd->bhqk', q[:, lo:hi].astype(jnp.float32),
                       k[:, :hi].astype(jnp.float32)) * scale
        strict = jnp.arange(hi)[None, :] < jnp.arange(lo, hi)[:, None]
        log_beta = jax.nn.log_sigmoid(z)
        log_1m_beta = jnp.where(strict, jax.nn.log_sigmoid(-z), 0.0)
        between = lax.cumsum(log_1m_beta, axis=3, reverse=True) - log_1m_beta
        a = jnp.where(strict, jnp.exp(log_beta + between), 0.0)
        o = jnp.einsum('bhqk,bkhd->bqhd', a, v[:, :hi].astype(jnp.float32))
        outs.append(o.astype(xn.dtype))
    o = jnp.concatenate(outs, axis=1).reshape(b, s, SB_WIDTH)
    return (o * jax.nn.silu(gate)) @ w_out


def short_conv_mixer(xn, w_in, conv_w, conv_b, w_out):
    b_gate, c_gate, u, gate = jnp.split(xn @ w_in, 4, axis=-1)
    cu = c_gate * u
    y = lax.conv_general_dilated(cu, conv_w[:, None, :], window_strides=(1,),
                                 padding=[(CONV_WIDTH - 1, 0)],
                                 dimension_numbers=('NWC', 'WIO', 'NWC'),
                                 feature_group_count=SC_WIDTH) + conv_b
    return (b_gate * y * jax.nn.silu(gate)) @ w_out


def mla_mixer(xn, positions, w_in, q_norm, w_uq, kv_norm, w_ukv,
              qn_nope, qn_rope, kn_nope, kn_rope, w_out):
    b, s, _ = xn.shape
    c_q, c_kv, k_rope, gate = jnp.split(
        xn @ w_in, [MLA_Q_RANK, MLA_Q_RANK + MLA_KV_RANK, MLA_Q_RANK + MLA_KV_RANK + MLA_ROPE_DIM], axis=-1)
    q = (rmsnorm(c_q, q_norm) @ w_uq).reshape(b, s, MLA_HEADS, MLA_NOPE_DIM + MLA_ROPE_DIM)
    kv = (rmsnorm(c_kv, kv_norm) @ w_ukv).reshape(b, s, MLA_HEADS, MLA_NOPE_DIM + MLA_V_DIM)
    q_nope = rmsnorm(q[..., :MLA_NOPE_DIM], qn_nope)
    q_rot = rope(rmsnorm(q[..., MLA_NOPE_DIM:], qn_rope), positions)
    k_nope = rmsnorm(kv[..., :MLA_NOPE_DIM], kn_nope)
    v = kv[..., MLA_NOPE_DIM:]
    k_rot = rope(rmsnorm(k_rope, kn_rope)[:, :, None, :], positions)[:, :, 0]
    scale = 1.0 / math.sqrt(MLA_NOPE_DIM + MLA_ROPE_DIM)
    outs = []
    for blk in range(s // Q_BLOCK):
        lo, hi = blk * Q_BLOCK, (blk + 1) * Q_BLOCK
        sc = (jnp.einsum('bqhd,bkhd->bhqk', q_nope[:, lo:hi].astype(jnp.float32),
                         k_nope[:, :hi].astype(jnp.float32))
              + jnp.einsum('bqhr,bkr->bhqk', q_rot[:, lo:hi].astype(jnp.float32),
                           k_rot[:, :hi].astype(jnp.float32))) * scale
        chunk_ok = (jnp.arange(hi)[None, :] // CHUNK) <= (jnp.arange(lo, hi)[:, None] // CHUNK)
        p = jax.nn.softmax(jnp.where(chunk_ok, sc, -jnp.inf), axis=-1)
        o = jnp.einsum('bhqk,bkhd->bqhd', p, v[:, :hi].astype(jnp.float32))
        outs.append(o.astype(xn.dtype))
    o = jnp.concatenate(outs, axis=1).reshape(b, s, MLA_WIDTH)
    return (o * jax.nn.silu(gate)) @ w_out


def retention_mixer(xn, positions, w_in, out_norm, w_out):
    b, s, _ = xn.shape
    q, k, v, gate = jnp.split(
        xn @ w_in, [RET_QK_WIDTH, 2 * RET_QK_WIDTH, 2 * RET_QK_WIDTH + RET_V_WIDTH], axis=-1)
    q = rope(q.reshape(b, s, RET_HEADS, RET_QK_DIM), positions)
    k = rope(k.reshape(b, s, RET_HEADS, RET_QK_DIM), positions) * (RET_QK_DIM ** -0.5)
    v = v.reshape(b, s, RET_HEADS, RET_V_DIM)
    n_chunks = s // CHUNK

    def to_chunks(t):
        return t.reshape(b, n_chunks, CHUNK, RET_HEADS, t.shape[-1]).transpose(1, 0, 3, 2, 4).astype(jnp.float32)

    log_gamma = jnp.log1p(-jnp.exp2(-5.0 - jnp.arange(RET_HEADS, dtype=jnp.float32)))
    n = jnp.arange(CHUNK, dtype=jnp.float32)
    rel = n[:, None] - n[None, :]
    decay_mask = jnp.where(rel >= 0, jnp.exp(log_gamma[:, None, None] * jnp.maximum(rel, 0.0)), 0.0)
    q_decay = jnp.exp(log_gamma[:, None] * (n + 1.0))[..., None]
    k_decay = jnp.exp(log_gamma[:, None] * (CHUNK - 1.0 - n))[..., None]
    chunk_decay = jnp.exp(log_gamma * CHUNK)[:, None, None]

    def step(state, qkv):
        qc, kc, vc = qkv
        intra = jnp.einsum('bhnd,bhmd->bhnm', qc, kc) * decay_mask
        out = (jnp.einsum('bhnm,bhmv->bhnv', intra, vc)
               + jnp.einsum('bhnd,bhdv->bhnv', qc, state) * q_decay)
        state = state * chunk_decay + jnp.einsum('bhmd,bhmv->bhdv', kc * k_decay, vc)
        return state, out

    state0 = jnp.zeros((b, RET_HEADS, RET_QK_DIM, RET_V_DIM), jnp.float32)
    _, o = lax.scan(step, state0, (to_chunks(q), to_chunks(k), to_chunks(v)))
    o = o.transpose(1, 0, 3, 2, 4).reshape(b, s, RET_HEADS, RET_V_DIM)
    o = rmsnorm(o, out_norm.reshape(RET_HEADS, RET_V_DIM)).astype(xn.dtype).reshape(b, s, RET_V_WIDTH)
    return (jax.nn.silu(gate) * o) @ w_out


def setup_inputs(seed: int = 0) -> dict:
    key = jax.random.key(seed)
    keys = iter(jax.random.split(key, 32))

    def dense(shape, fan_in):
        return jax.random.normal(next(keys), shape, jnp.float32) * (fan_in ** -0.5)

    def gain(shape):
        return 1.0 + 0.02 * jax.random.normal(next(keys), shape, jnp.float32)

    na, nb, nc, nd = (n_layers_of(m) for m in range(N_MIXERS))
    x = jax.random.normal(next(keys), (BATCH, SEQ, D_MODEL), jnp.float32)
    offsets = jax.random.randint(next(keys), (BATCH, 1), 0, 4096, dtype=jnp.int32)
    positions = (offsets + jnp.arange(SEQ, dtype=jnp.int32)[None, :]).astype(jnp.int32)
    return {
        'x': x,
        'positions': positions,
        'norm_g': gain((DEPTH, D_MODEL)),
        'sb_w_in': dense((na, D_MODEL, 4 * SB_WIDTH), D_MODEL),
        'sb_w_out': dense((na, SB_WIDTH, D_MODEL), SB_WIDTH),
        'sc_w_in': dense((nb, D_MODEL, 4 * SC_WIDTH), D_MODEL),
        'sc_conv_w': dense((nb, CONV_WIDTH, SC_WIDTH), CONV_WIDTH),
        'sc_conv_b': 0.02 * jax.random.normal(next(keys), (nb, SC_WIDTH), jnp.float32),
        'sc_w_out': dense((nb, SC_WIDTH, D_MODEL), SC_WIDTH),
        'mla_w_in': dense((nc, D_MODEL, MLA_IN_WIDTH), D_MODEL),
        'mla_q_norm': gain((nc, MLA_Q_RANK)),
        'mla_w_uq': dense((nc, MLA_Q_RANK, MLA_HEADS * (MLA_NOPE_DIM + MLA_ROPE_DIM)), MLA_Q_RANK),
        'mla_kv_norm': gain((nc, MLA_KV_RANK)),
        'mla_w_ukv': dense((nc, MLA_KV_RANK, MLA_HEADS * (MLA_NOPE_DIM + MLA_V_DIM)), MLA_KV_RANK),
        'mla_qn_nope': gain((nc, MLA_NOPE_DIM)),
        'mla_qn_rope': gain((nc, MLA_ROPE_DIM)),
        'mla_kn_nope': gain((nc, MLA_NOPE_DIM)),
        'mla_kn_rope': gain((nc, MLA_ROPE_DIM)),
        'mla_w_out': dense((nc, MLA_WIDTH, D_MODEL), MLA_WIDTH),
        'ret_w_in': dense((nd, D_MODEL, 2 * RET_QK_WIDTH + 2 * RET_V_WIDTH), D_MODEL),
        'ret_out_norm': gain((nd, RET_V_WIDTH)),
        'ret_w_out': dense((nd, RET_V_WIDTH, D_MODEL), RET_V_WIDTH),
    }


def reference(x, positions, norm_g, sb_w_in, sb_w_out, sc_w_in, sc_conv_w, sc_conv_b, sc_w_out,
              mla_w_in, mla_q_norm, mla_w_uq, mla_kv_norm, mla_w_ukv, mla_qn_nope, mla_qn_rope,
              mla_kn_nope, mla_kn_rope, mla_w_out, ret_w_in, ret_out_norm, ret_w_out):
    for i in range(DEPTH):
        kind, j = i % N_MIXERS, i // N_MIXERS
        xn = rmsnorm(x, norm_g[i])
        if kind == 0:
            y = stick_breaking_mixer(xn, sb_w_in[j], sb_w_out[j])
        elif kind == 1:
            y = short_conv_mixer(xn, sc_w_in[j], sc_conv_w[j], sc_conv_b[j], sc_w_out[j])
        elif kind == 2:
            y = mla_mixer(xn, positions, mla_w_in[j], mla_q_norm[j], mla_w_uq[j], mla_kv_norm[j],
                          mla_w_ukv[j], mla_qn_nope[j], mla_qn_rope[j], mla_kn_nope[j],
                          mla_kn_rope[j], mla_w_out[j])
        else:
            y = retention_mixer(xn, positions, ret_w_in[j], ret_out_norm[j], ret_w_out[j])
        x = x + y.astype(x.dtype)
    return x
```

```python
import functools
import math

import jax
import jax.numpy as jnp
from jax import lax
from jax.experimental import pallas as pl
from jax.experimental.pallas import tpu as pltpu

F32 = jnp.float32
BF16 = jnp.bfloat16

D_MODEL = 2048
EPS = 1e-6
ROPE_BASE = 10000.0
CHUNK = 64

LANES = 128
SB_HEADS, SB_DIM = 16, 128
SC_WIDTH = 2048
MLA_HEADS, MLA_NOPE, MLA_ROPE, MLA_V = 16, 128, 64, 128
MLA_Q_RANK, MLA_KV_RANK = 512, 256
MLA_QK_PAD = 256
RET_HEADS, RET_QK, RET_V = 8, 256, 512

VMEM_LIMIT = 56 * 1024 * 1024

EXP_ZERO_BELOW = -104.0


def _cparams(sem):
    return pltpu.CompilerParams(dimension_semantics=sem, vmem_limit_bytes=VMEM_LIMIT)


def _silu(x):
    return x * (1.0 / (1.0 + jnp.exp(-x)))


def _norm_matmul_kernel(x_ref, g_ref, w_ref, o_ref, xn_ref):
    @pl.when(pl.program_id(1) == 0)
    def _():
        x = x_ref[...]
        ms = jnp.mean(x * x, axis=-1, keepdims=True)
        xn_ref[...] = (x * lax.rsqrt(ms + EPS) * g_ref[...]).astype(BF16)

    o_ref[...] = jnp.dot(xn_ref[...], w_ref[...], preferred_element_type=F32).astype(o_ref.dtype)


def norm_matmul(x, g, w, *, tm=1024, tn=1024):
    n, d = x.shape
    dout = w.shape[1]
    tm, tn = min(tm, n), min(tn, dout)
    return pl.pallas_call(
        _norm_matmul_kernel,
        out_shape=jax.ShapeDtypeStruct((n, dout), BF16),
        grid=(n // tm, dout // tn),
        in_specs=[pl.BlockSpec((tm, d), lambda i, j: (i, 0)),
                  pl.BlockSpec((1, d), lambda i, j: (0, 0)),
                  pl.BlockSpec((d, tn), lambda i, j: (0, j))],
        out_specs=pl.BlockSpec((tm, tn), lambda i, j: (i, j)),
        scratch_shapes=[pltpu.VMEM((tm, d), BF16)],
        compiler_params=_cparams(("arbitrary", "arbitrary")),
        name="norm_matmul",
    )(x, g.reshape(1, d), w)


def _matmul_residual_kernel(z_ref, w_ref, x_ref, o_ref):
    o_ref[...] = x_ref[...] + jnp.dot(z_ref[...], w_ref[...], preferred_element_type=F32)


def matmul_residual(z, w, x, *, tm=1024, tn=1024):
    n, k = z.shape
    dout = w.shape[1]
    tm, tn = min(tm, n), min(tn, dout)
    return pl.pallas_call(
        _matmul_residual_kernel,
        out_shape=jax.ShapeDtypeStruct((n, dout), F32),
        grid=(n // tm, dout // tn),
        in_specs=[pl.BlockSpec((tm, k), lambda i, j: (i, 0)),
                  pl.BlockSpec((k, tn), lambda i, j: (0, j)),
                  pl.BlockSpec((tm, tn), lambda i, j: (i, j))],
        out_specs=pl.BlockSpec((tm, tn), lambda i, j: (i, j)),
        compiler_params=_cparams(("arbitrary", "arbitrary")),
        name="matmul_residual",
    )(z, w, x)


def _rope_table_kernel(pos_ref, cos_ref, sin_ref, *, half):
    lane = lax.broadcasted_iota(jnp.int32, (1, LANES), 1) % half
    inv_freq = jnp.exp(lane.astype(F32) * (-math.log(ROPE_BASE) / half))
    ang = pos_ref[...] * inv_freq
    cos_ref[...] = jnp.cos(ang)
    sin_ref[...] = jnp.sin(ang)


def rope_tables(pos, half, *, tm=1024):
    n = pos.shape[0]
    tm = min(tm, n)
    spec = pl.BlockSpec((tm, LANES), lambda i: (i, 0))
    return pl.pallas_call(
        functools.partial(_rope_table_kernel, half=half),
        out_shape=[jax.ShapeDtypeStruct((n, LANES), F32)] * 2,
        grid=(n // tm,),
        in_specs=[pl.BlockSpec((tm, 1), lambda i: (i, 0))],
        out_specs=[spec, spec],
        compiler_params=_cparams(("arbitrary",)),
        name="rope_tables",
    )(pos)


def _sb_kernel(q_ref, k_ref, v_ref, g_ref, o_ref, acc_ref, r_ref, *, tq, tk, scale):
    qi = pl.program_id(2)
    q = q_ref[...]
    acc_ref[...] = jnp.zeros_like(acc_ref)
    r_ref[...] = jnp.zeros_like(r_ref)

    row = lax.broadcasted_iota(jnp.int32, (tk, tk), 0)
    col = lax.broadcasted_iota(jnp.int32, (tk, tk), 1)
    later = (row > col).astype(BF16)
    t_idx = qi * tq + lax.broadcasted_iota(jnp.int32, (tq, tk), 0)
    s_loc = lax.broadcasted_iota(jnp.int32, (tq, tk), 1)

    def cond(carry):
        kb, rmax = carry
        return jnp.logical_and(kb >= 0, rmax >= EXP_ZERO_BELOW)

    def body(carry):
        kb, _ = carry
        ks = pl.multiple_of(kb * tk, tk)
        k = k_ref[pl.ds(ks, tk), :]
        v = v_ref[pl.ds(ks, tk), :]
        z = lax.dot_general(q, k, (((1,), (1,)), ((), ())), preferred_element_type=F32) * scale
        strict = (s_loc + ks) < t_idx
        softplus_tail = jnp.log1p(jnp.exp(-jnp.abs(z)))
        log_beta = jnp.minimum(z, 0.0) - softplus_tail
        log_1m = jnp.where(strict, log_beta - z, 0.0)
        hi = log_1m.astype(BF16)
        lo = (log_1m - hi.astype(F32)).astype(BF16)
        after = (jnp.dot(hi, later, preferred_element_type=F32)
                 + jnp.dot(lo, later, preferred_element_type=F32))
        r = r_ref[...]
        a = jnp.where(strict, jnp.exp(log_beta + after + r), 0.0)
        acc_ref[...] += jnp.dot(a.astype(BF16), v, preferred_element_type=F32)
        r_new = r + jnp.sum(log_1m, axis=1, keepdims=True)
        r_ref[...] = r_new
        return kb - 1, jnp.max(r_new)

    lax.while_loop(cond, body, (qi * (tq // tk) + (tq // tk - 1), jnp.float32(0.0)))
    o_ref[...] = (acc_ref[...] * _silu(g_ref[...].astype(F32))).astype(o_ref.dtype)


def sb_attention(h, batch, seq, *, tq=256, tk=256):
    n = h.shape[0]
    tq, tk = min(tq, seq), min(tk, seq)
    nq = seq // tq
    nh = SB_HEADS
    kernel = functools.partial(_sb_kernel, tq=tq, tk=tk, scale=1.0 / math.sqrt(SB_DIM))
    return pl.pallas_call(
        kernel,
        out_shape=jax.ShapeDtypeStruct((n, nh * SB_DIM), BF16),
        grid=(batch, nh, nq),
        in_specs=[pl.BlockSpec((tq, SB_DIM), lambda b, hh, i: (b * nq + i, hh)),
                  pl.BlockSpec((seq, SB_DIM), lambda b, hh, i: (b, nh + hh)),
                  pl.BlockSpec((seq, SB_DIM), lambda b, hh, i: (b, 2 * nh + hh)),
                  pl.BlockSpec((tq, SB_DIM), lambda b, hh, i: (b * nq + i, 3 * nh + hh))],
        out_specs=pl.BlockSpec((tq, SB_DIM), lambda b, hh, i: (b * nq + i, hh)),
        scratch_shapes=[pltpu.VMEM((tq, SB_DIM), F32), pltpu.VMEM((tq, 1), F32)],
        compiler_params=_cparams(("arbitrary", "arbitrary", "arbitrary")),
        name="sb_attention",
    )(h, h, h, h)


def _conv_kernel(b_ref, c_ref, u_ref, g_ref, w_ref, bias_ref, o_ref, halo_ref, *, tm, blocks_per_seq):
    i = pl.program_id(0)
    @pl.when(i % blocks_per_seq == 0)
    def _():
        halo_ref[...] = jnp.zeros_like(halo_ref)

    cu = c_ref[...].astype(F32) * u_ref[...].astype(F32)
    prev = halo_ref[...]
    row = lax.broadcasted_iota(jnp.int32, (tm, 1), 0)
    back1 = jnp.where(row == 0, prev[7:8, :], pltpu.roll(cu, 1, axis=0))
    back2 = jnp.where(row == 0, prev[6:7, :],
                      jnp.where(row == 1, prev[7:8, :], pltpu.roll(cu, 2, axis=0)))
    w = w_ref[...]
    y = w[0:1, :] * back2 + w[1:2, :] * back1 + w[2:3, :] * cu + bias_ref[...]
    halo_ref[...] = cu[tm - 8:, :]
    o_ref[...] = (b_ref[...].astype(F32) * y * _silu(g_ref[...].astype(F32))).astype(o_ref.dtype)


def short_conv(h, conv_w, conv_b, seq, *, tm=256):
    n = h.shape[0]
    w = SC_WIDTH
    tm = min(tm, seq)
    kernel = functools.partial(_conv_kernel, tm=tm, blocks_per_seq=seq // tm)
    col = lambda j: pl.BlockSpec((tm, w), lambda i: (i, j))
    return pl.pallas_call(
        kernel,
        out_shape=jax.ShapeDtypeStruct((n, w), BF16),
        grid=(n // tm,),
        in_specs=[col(0), col(1), col(2), col(3),
                  pl.BlockSpec((3, w), lambda i: (0, 0)),
                  pl.BlockSpec((1, w), lambda i: (0, 0))],
        out_specs=pl.BlockSpec((tm, w), lambda i: (i, 0)),
        scratch_shapes=[pltpu.VMEM((8, w), F32)],
        compiler_params=_cparams(("arbitrary",)),
        name="short_conv",
    )(h, h, h, h, conv_w, conv_b.reshape(1, w))


def _rms(x, width):
    return lax.rsqrt(jnp.sum(x * x, axis=-1, keepdims=True) * (1.0 / width) + EPS)


def _mla_prep_kernel(c_ref, wq_ref, wkv_ref, qg_ref, kvg_ref, qn_ref, qr_ref, qrs_ref,
                     kn_ref, kr_ref, krs_ref, cos_ref, sin_ref,
                     q_out, k_out, v_out, *, scale):
    c = c_ref[...].astype(F32)
    c_q = c[:, :MLA_Q_RANK]
    c_kv = c[:, MLA_Q_RANK:MLA_Q_RANK + MLA_KV_RANK]
    kro = c[:, MLA_Q_RANK + MLA_KV_RANK:MLA_Q_RANK + MLA_KV_RANK + LANES]
    kro_s = c[:, MLA_Q_RANK + MLA_KV_RANK + LANES:]
    cqn = (c_q * _rms(c_q, MLA_Q_RANK) * qg_ref[...]).astype(BF16)
    ckn = (c_kv * _rms(c_kv, MLA_KV_RANK) * kvg_ref[...]).astype(BF16)

    cos = cos_ref[...]
    sin = sin_ref[...]
    lane = lax.broadcasted_iota(jnp.int32, (1, LANES), 1)
    sin_signed = jnp.where(lane < MLA_ROPE // 2, -sin, sin)

    k_rot = (kro * kr_ref[...] * cos + kro_s * krs_ref[...] * sin_signed) * _rms(kro, MLA_ROPE)
    k_rot = k_rot.astype(k_out.dtype)

    for hh in range(MLA_HEADS):
        q3 = jnp.dot(cqn, wq_ref[:, hh * 3 * LANES:(hh + 1) * 3 * LANES], preferred_element_type=F32)
        qn = q3[:, :LANES]
        qr = q3[:, LANES:2 * LANES]
        qs = q3[:, 2 * LANES:]
        qn = qn * _rms(qn, MLA_NOPE) * qn_ref[...]
        q_rot = (qr * qr_ref[...] * cos + qs * qrs_ref[...] * sin_signed) * _rms(qr, MLA_ROPE)
        q_out[:, hh * MLA_QK_PAD:hh * MLA_QK_PAD + LANES] = (qn * scale).astype(q_out.dtype)
        q_out[:, hh * MLA_QK_PAD + LANES:(hh + 1) * MLA_QK_PAD] = (q_rot * scale).astype(q_out.dtype)

        kv = jnp.dot(ckn, wkv_ref[:, hh * 2 * LANES:(hh + 1) * 2 * LANES], preferred_element_type=F32)
        kn = kv[:, :LANES]
        kn = kn * _rms(kn, MLA_NOPE) * kn_ref[...]
        k_out[:, hh * MLA_QK_PAD:hh * MLA_QK_PAD + LANES] = kn.astype(k_out.dtype)
        k_out[:, hh * MLA_QK_PAD + LANES:(hh + 1) * MLA_QK_PAD] = k_rot
        v_out[:, hh * MLA_V:(hh + 1) * MLA_V] = kv[:, LANES:].astype(v_out.dtype)


def mla_prep(h, wq, wkv, gains, cos, sin, *, tm=256):
    n = h.shape[0]
    tm = min(tm, n)
    cw = 1024
    row = lambda width, j=0: pl.BlockSpec((tm, width), lambda i: (i, j))
    full = lambda a: pl.BlockSpec(a.shape, lambda i: (0, 0))
    kernel = functools.partial(_mla_prep_kernel, scale=1.0 / math.sqrt(MLA_NOPE + MLA_ROPE))
    return pl.pallas_call(
        kernel,
        out_shape=[jax.ShapeDtypeStruct((n, MLA_HEADS * MLA_QK_PAD), BF16),
                   jax.ShapeDtypeStruct((n, MLA_HEADS * MLA_QK_PAD), BF16),
                   jax.ShapeDtypeStruct((n, MLA_HEADS * MLA_V), BF16)],
        grid=(n // tm,),
        in_specs=[row(cw, 2), full(wq), full(wkv)] + [full(a) for a in gains] + [row(LANES), row(LANES)],
        out_specs=[row(MLA_HEADS * MLA_QK_PAD), row(MLA_HEADS * MLA_QK_PAD), row(MLA_HEADS * MLA_V)],
        compiler_params=_cparams(("arbitrary",)),
        name="mla_prep",
    )(h, wq, wkv, *gains, cos, sin)


def _mla_attn_kernel(q_ref, k_ref, v_ref, g_ref, o_ref, m_ref, l_ref, acc_ref, *, tq, tk):
    qi = pl.program_id(2)
    q = q_ref[...]
    m_ref[...] = jnp.full_like(m_ref, -jnp.inf)
    l_ref[...] = jnp.zeros_like(l_ref)
    acc_ref[...] = jnp.zeros_like(acc_ref)

    def block(kb, masked):
        ks = pl.multiple_of(kb * tk, tk)
        k = k_ref[pl.ds(ks, tk), :]
        v = v_ref[pl.ds(ks, tk), :]
        s = lax.dot_general(q, k, (((1,), (1,)), ((), ())), preferred_element_type=F32)
        if masked:
            t_chunk = lax.broadcasted_iota(jnp.int32, (tq, tk), 0) // CHUNK
            s_chunk = lax.broadcasted_iota(jnp.int32, (tq, tk), 1) // CHUNK
            s = jnp.where(s_chunk <= t_chunk, s, -jnp.inf)
        m_old = m_ref[...]
        m_new = jnp.maximum(m_old, jnp.max(s, axis=1, keepdims=True))
        alpha = jnp.exp(m_old - m_new)
        p = jnp.exp(s - m_new)
        l_ref[...] = alpha * l_ref[...] + jnp.sum(p, axis=1, keepdims=True)
        acc_ref[...] = alpha * acc_ref[...] + jnp.dot(p.astype(BF16), v, preferred_element_type=F32)
        m_ref[...] = m_new

    def body(kb, carry):
        block(kb, False)
        return carry

    lax.fori_loop(0, qi, body, 0)
    block(qi, True)
    o = acc_ref[...] * (1.0 / l_ref[...])
    o_ref[...] = (o * _silu(g_ref[...].astype(F32))).astype(o_ref.dtype)


def mla_attention(qf, kf, v, h, batch, seq, *, t=256):
    n = qf.shape[0]
    t = min(t, seq)
    nq = seq // t
    kernel = functools.partial(_mla_attn_kernel, tq=t, tk=t)
    return pl.pallas_call(
        kernel,
        out_shape=jax.ShapeDtypeStruct((n, MLA_HEADS * MLA_V), BF16),
        grid=(batch, MLA_HEADS, nq),
        in_specs=[pl.BlockSpec((t, MLA_QK_PAD), lambda b, hh, i: (b * nq + i, hh)),
                  pl.BlockSpec((seq, MLA_QK_PAD), lambda b, hh, i: (b, hh)),
                  pl.BlockSpec((seq, MLA_V), lambda b, hh, i: (b, hh)),
                  pl.BlockSpec((t, MLA_V), lambda b, hh, i: (b * nq + i, hh))],
        out_specs=pl.BlockSpec((t, MLA_V), lambda b, hh, i: (b * nq + i, hh)),
        scratch_shapes=[pltpu.VMEM((t, 1), F32), pltpu.VMEM((t, 1), F32), pltpu.VMEM((t, MLA_V), F32)],
        compiler_params=_cparams(("arbitrary", "arbitrary", "arbitrary")),
        name="mla_attention",
    )(qf, kf, v, h)


def _ret_kernel(q_ref, k_ref, v_ref, g_ref, cos_ref, sin_ref, on_ref, o_ref,
                state_ref, decay_ref, qd_ref, kd_ref, cd_ref, *, c):
    hh = pl.program_id(1)
    ci = pl.program_id(2)

    @pl.when(ci == 0)
    def _():
        state_ref[...] = jnp.zeros_like(state_ref)
        head = jnp.full((1, 1), hh, jnp.int32).astype(F32)
        log_gamma = jnp.log1p(-jnp.exp2(-5.0 - head))
        n_row = lax.broadcasted_iota(jnp.int32, (c, c), 0)
        n_col = lax.broadcasted_iota(jnp.int32, (c, c), 1)
        rel = (n_row - n_col).astype(F32)
        decay_ref[...] = jnp.where(rel >= 0, jnp.exp(log_gamma * jnp.maximum(rel, 0.0)), 0.0)
        n1 = lax.broadcasted_iota(jnp.int32, (c, 1), 0).astype(F32)
        qd_ref[...] = jnp.exp(log_gamma * (n1 + 1.0))
        kd_ref[...] = jnp.exp(log_gamma * (c - 1.0 - n1))
        cd_ref[...] = jnp.exp(log_gamma * c)

    cos = cos_ref[...]
    sin = sin_ref[...]
    half = RET_QK // 2

    def rope(x):
        x1, x2 = x[:, :half], x[:, half:]
        return jnp.concatenate([x1 * cos - x2 * sin, x1 * sin + x2 * cos], axis=1)

    q = rope(q_ref[...].astype(F32))
    k = rope(k_ref[...].astype(F32)) * (RET_QK ** -0.5)
    v = v_ref[...]
    state = state_ref[...]

    intra = lax.dot_general(q.astype(BF16), k.astype(BF16), (((1,), (1,)), ((), ())),
                            preferred_element_type=F32) * decay_ref[...]
    out = (jnp.dot(intra.astype(BF16), v, preferred_element_type=F32)
           + jnp.dot((q * qd_ref[...]).astype(BF16), state.astype(BF16), preferred_element_type=F32))
    kt = (k * kd_ref[...]).T.astype(BF16)
    state_ref[...] = state * cd_ref[...] + jnp.dot(kt, v, preferred_element_type=F32)

    normed = out * _rms(out, RET_V) * on_ref[...]
    o_ref[...] = (_silu(g_ref[...].astype(F32)) * normed).astype(o_ref.dtype)


def retention(h, cos, sin, out_norm, batch, seq, *, c=256):
    n = h.shape[0]
    c = min(c, seq)
    nc = seq // c
    nh = RET_HEADS
    kernel = functools.partial(_ret_kernel, c=c)
    return pl.pallas_call(
        kernel,
        out_shape=jax.ShapeDtypeStruct((n, nh * RET_V), BF16),
        grid=(batch, nh, nc),
        in_specs=[pl.BlockSpec((c, RET_QK), lambda b, hh, i: (b * nc + i, hh)),
                  pl.BlockSpec((c, RET_QK), lambda b, hh, i: (b * nc + i, nh + hh)),
                  pl.BlockSpec((c, RET_V), lambda b, hh, i: (b * nc + i, nh + hh)),
                  pl.BlockSpec((c, RET_V), lambda b, hh, i: (b * nc + i, 2 * nh + hh)),
                  pl.BlockSpec((c, LANES), lambda b, hh, i: (b * nc + i, 0)),
                  pl.BlockSpec((c, LANES), lambda b, hh, i: (b * nc + i, 0)),
                  pl.BlockSpec((1, RET_V), lambda b, hh, i: (0, hh))],
        out_specs=pl.BlockSpec((c, RET_V), lambda b, hh, i: (b * nc + i, hh)),
        scratch_shapes=[pltpu.VMEM((RET_QK, RET_V), F32), pltpu.VMEM((c, c), F32),
                        pltpu.VMEM((c, 1), F32), pltpu.VMEM((c, 1), F32), pltpu.VMEM((1, 1), F32)],
        compiler_params=_cparams(("arbitrary", "arbitrary", "arbitrary")),
        name="retention",
    )(h, h, h, h, cos, sin, out_norm.reshape(1, nh * RET_V))


def _rope_cols(w):
    half = MLA_ROPE // 2
    x1, x2 = w[..., :half], w[..., half:]
    z = jnp.zeros(w.shape[:-1] + (LANES - MLA_ROPE,), w.dtype)
    return jnp.concatenate([x1, x2, z, x2, x1, z], axis=-1)


def _mla_weights(w_in, w_uq, w_ukv):
    q_end = MLA_Q_RANK
    kv_end = q_end + MLA_KV_RANK
    r_end = kv_end + MLA_ROPE
    w_in_x = jnp.concatenate([w_in[:, r_end:], w_in[:, :kv_end], _rope_cols(w_in[:, kv_end:r_end])], axis=1)
    uq = w_uq.reshape(MLA_Q_RANK, MLA_HEADS, MLA_NOPE + MLA_ROPE)
    wq = jnp.concatenate([uq[..., :MLA_NOPE], _rope_cols(uq[..., MLA_NOPE:])], axis=-1)
    wq = wq.reshape(MLA_Q_RANK, MLA_HEADS * 3 * LANES)
    return w_in_x.astype(BF16), wq.astype(BF16), w_ukv.astype(BF16)


def _rope_gain(g):
    both = _rope_cols(g.reshape(1, MLA_ROPE))
    return both[:, :LANES], both[:, LANES:]


def kernel(x, positions, norm_g, sb_w_in, sb_w_out, sc_w_in, sc_conv_w, sc_conv_b, sc_w_out,
           mla_w_in, mla_q_norm, mla_w_uq, mla_kv_norm, mla_w_ukv, mla_qn_nope, mla_qn_rope,
           mla_kn_nope, mla_kn_rope, mla_w_out, ret_w_in, ret_out_norm, ret_w_out):
    batch, seq, d = x.shape
    n = batch * seq
    depth = norm_g.shape[0]
    xs = x.reshape(n, d)
    pos = positions.reshape(n, 1).astype(F32)
    mla_cos, mla_sin = rope_tables(pos, MLA_ROPE // 2)
    ret_cos, ret_sin = rope_tables(pos, RET_QK // 2)

    for i in range(depth):
        kind, j = i % 4, i // 4
        if kind == 0:
            h = norm_matmul(xs, norm_g[i], sb_w_in[j].astype(BF16))
            z = sb_attention(h, batch, seq)
            w_out = sb_w_out[j]
        elif kind == 1:
            h = norm_matmul(xs, norm_g[i], sc_w_in[j].astype(BF16))
            z = short_conv(h, sc_conv_w[j], sc_conv_b[j], seq)
            w_out = sc_w_out[j]
        elif kind == 2:
            w_in_x, wq, wkv = _mla_weights(mla_w_in[j], mla_w_uq[j], mla_w_ukv[j])
            h = norm_matmul(xs, norm_g[i], w_in_x)
            qr, qrs = _rope_gain(mla_qn_rope[j])
            kr, krs = _rope_gain(mla_kn_rope[j])
            gains = [mla_q_norm[j].reshape(1, -1), mla_kv_norm[j].reshape(1, -1),
                     mla_qn_nope[j].reshape(1, -1), qr, qrs, mla_kn_nope[j].reshape(1, -1), kr, krs]
            qf, kf, v = mla_prep(h, wq, wkv, gains, mla_cos, mla_sin)
            z = mla_attention(qf, kf, v, h, batch, seq)
            w_out = mla_w_out[j]
        else:
            h = norm_matmul(xs, norm_g[i], ret_w_in[j].astype(BF16))
            z = retention(h, ret_cos, ret_sin, ret_out_norm[j], batch, seq)
            w_out = ret_w_out[j]
        xs = matmul_residual(z, w_out.astype(BF16), xs, tn=1024 if z.shape[1] <= 2048 else 512)
    return xs.reshape(batch, seq, d)
```

```python
import functools
import math

import jax
import jax.numpy as jnp
from jax import lax
from jax.experimental import pallas as pl
from jax.experimental.pallas import tpu as pltpu

F32 = jnp.float32
BF16 = jnp.bfloat16

D_MODEL = 2048
EPS = 1e-6
ROPE_BASE = 10000.0
CHUNK = 64

LANES = 128
SB_HEADS, SB_DIM = 16, 128
SC_WIDTH = 2048
MLA_HEADS, MLA_NOPE, MLA_ROPE, MLA_V = 16, 128, 64, 128
MLA_Q_RANK, MLA_KV_RANK = 512, 256
MLA_QK_PAD = 256
MLA_TK = 512
RET_HEADS, RET_QK, RET_V = 8, 256, 512

VMEM_LIMIT = 56 * 1024 * 1024

SB_T = 256
SB_GROUPS = 4
EXP2_ZERO_BELOW = -150.0


def _cparams(sem):
    return pltpu.CompilerParams(dimension_semantics=sem, vmem_limit_bytes=VMEM_LIMIT)


def _silu(x):
    return x * (1.0 / (1.0 + jnp.exp(-x)))


def _norm_matmul_kernel(x_ref, g_ref, w_ref, o_ref, xn_ref):
    @pl.when(pl.program_id(1) == 0)
    def _():
        x = x_ref[...]
        ms = jnp.mean(x * x, axis=-1, keepdims=True)
        xn_ref[...] = (x * lax.rsqrt(ms + EPS) * g_ref[...]).astype(BF16)

    o_ref[...] = jnp.dot(xn_ref[...], w_ref[...], preferred_element_type=F32).astype(o_ref.dtype)


def norm_matmul(x, g, w, *, tm=1024, tn=1024):
    n, d = x.shape
    dout = w.shape[1]
    tm, tn = min(tm, n), min(tn, dout)
    return pl.pallas_call(
        _norm_matmul_kernel,
        out_shape=jax.ShapeDtypeStruct((n, dout), BF16),
        grid=(n // tm, dout // tn),
        in_specs=[pl.BlockSpec((tm, d), lambda i, j: (i, 0)),
                  pl.BlockSpec((1, d), lambda i, j: (0, 0)),
                  pl.BlockSpec((d, tn), lambda i, j: (0, j))],
        out_specs=pl.BlockSpec((tm, tn), lambda i, j: (i, j)),
        scratch_shapes=[pltpu.VMEM((tm, d), BF16)],
        compiler_params=_cparams(("arbitrary", "arbitrary")),
        name="norm_matmul",
    )(x, g.reshape(1, d), w)


def _matmul_residual_kernel(z_ref, w_ref, x_ref, o_ref):
    o_ref[...] = x_ref[...] + jnp.dot(z_ref[...], w_ref[...], preferred_element_type=F32)


def matmul_residual(z, w, x, *, tm=1024, tn=1024):
    n, k = z.shape
    dout = w.shape[1]
    tm, tn = min(tm, n), min(tn, dout)
    return pl.pallas_call(
        _matmul_residual_kernel,
        out_shape=jax.ShapeDtypeStruct((n, dout), F32),
        grid=(n // tm, dout // tn),
        in_specs=[pl.BlockSpec((tm, k), lambda i, j: (i, 0)),
                  pl.BlockSpec((k, tn), lambda i, j: (0, j)),
                  pl.BlockSpec((tm, tn), lambda i, j: (i, j))],
        out_specs=pl.BlockSpec((tm, tn), lambda i, j: (i, j)),
        compiler_params=_cparams(("arbitrary", "arbitrary")),
        name="matmul_residual",
    )(z, w, x)


def _rope_table_kernel(pos_ref, cos_ref, sin_ref, *, half):
    lane = lax.broadcasted_iota(jnp.int32, (1, LANES), 1) % half
    inv_freq = jnp.exp(lane.astype(F32) * (-math.log(ROPE_BASE) / half))
    ang = pos_ref[...] * inv_freq
    cos_ref[...] = jnp.cos(ang)
    sin_ref[...] = jnp.sin(ang)


def rope_tables(pos, half, *, tm=1024):
    n = pos.shape[0]
    tm = min(tm, n)
    spec = pl.BlockSpec((tm, LANES), lambda i: (i, 0))
    return pl.pallas_call(
        functools.partial(_rope_table_kernel, half=half),
        out_shape=[jax.ShapeDtypeStruct((n, LANES), F32)] * 2,
        grid=(n // tm,),
        in_specs=[pl.BlockSpec((tm, 1), lambda i: (i, 0))],
        out_specs=[spec, spec],
        compiler_params=_cparams(("arbitrary",)),
        name="rope_tables",
    )(pos)


def _sb_kernel(q_ref, k_ref, v_ref, g_ref, o_ref, vt_ref, acc_ref, r_ref, *, t, groups):
    qi = pl.program_id(2)

    @pl.when(qi == 0)
    def _():
        for j in range(vt_ref.shape[0]):
            vt_ref[j] = v_ref[j * t:(j + 1) * t, :].astype(F32).T.astype(BF16)

    acc_ref[...] = jnp.zeros_like(acc_ref)
    r_ref[...] = jnp.zeros_like(r_ref)
    tq = groups * t
    key = lax.broadcasted_iota(jnp.int32, (t, t), 0)
    later = (lax.broadcasted_iota(jnp.int32, (t, t), 1) > key).astype(BF16)
    later2 = jnp.concatenate([later, later], axis=1)
    lane = lax.broadcasted_iota(jnp.int32, (1, tq), 1)
    first = groups * qi

    def fold(d, diag):
        blocks = [jnp.maximum(first + g - d, 0) for g in range(groups)]
        group_lanes = [slice(g * t, (g + 1) * t) for g in range(groups)]
        zs = [lax.dot_general(k_ref[pl.ds(pl.multiple_of(kb * t, t), t), :], q_ref[lanes, :],
                              (((1,), (1,)), ((), ())), preferred_element_type=F32)
              for kb, lanes in zip(blocks, group_lanes)]
        r = jnp.where(lane // t >= d - first, r_ref[...], -jnp.inf)
        strict = key < lax.broadcasted_iota(jnp.int32, (t, t), 1)

        log_betas, afters, heads = [], [], []
        for z in zs:
            log_beta = jnp.minimum(z, 0.0) - jnp.log2(1.0 + jnp.exp2(-jnp.abs(z)))
            log_1m = log_beta - z
            if diag:
                log_1m = jnp.where(strict, log_1m, 0.0)
            hi = log_1m.astype(BF16)
            lo = (log_1m - hi.astype(F32)).astype(BF16)
            afters.append(jnp.dot(later2, jnp.concatenate([hi, lo], axis=0), preferred_element_type=F32))
            log_betas.append(log_beta)
            heads.append(log_1m[0:1, :])

        r_new = []
        for g, lanes in enumerate(group_lanes):
            a = jnp.exp2(log_betas[g] + afters[g] + r[:, lanes])
            if diag:
                a = jnp.where(strict, a, 0.0)
            acc_ref[:, lanes] += jnp.dot(vt_ref[blocks[g]], a.astype(BF16), preferred_element_type=F32)
            r_new.append(r[:, lanes] + afters[g][0:1, :] + heads[g])
        r_new = jnp.concatenate(r_new, axis=1)
        r_ref[...] = r_new
        return jnp.max(r_new)

    def cond(carry):
        d, rmax = carry
        return jnp.logical_and(d <= first + groups - 1, rmax >= EXP2_ZERO_BELOW)

    def body(carry):
        d, _ = carry
        return d + 1, fold(d, False)

    lax.while_loop(cond, body, (jnp.int32(1), fold(0, True)))
    o_ref[...] = (acc_ref[...].T * _silu(g_ref[...].astype(F32))).astype(o_ref.dtype)


def sb_attention(h, batch, seq):
    n = h.shape[0]
    t = min(SB_T, seq)
    groups = min(SB_GROUPS, seq // t)
    tq = t * groups
    nq = seq // tq
    nh = SB_HEADS
    kernel = functools.partial(_sb_kernel, t=t, groups=groups)
    return pl.pallas_call(
        kernel,
        out_shape=jax.ShapeDtypeStruct((n, nh * SB_DIM), BF16),
        grid=(batch, nh, nq),
        in_specs=[pl.BlockSpec((tq, SB_DIM), lambda b, hh, i: (b * nq + i, hh)),
                  pl.BlockSpec((seq, SB_DIM), lambda b, hh, i: (b, nh + hh)),
                  pl.BlockSpec((seq, SB_DIM), lambda b, hh, i: (b, 2 * nh + hh)),
                  pl.BlockSpec((tq, SB_DIM), lambda b, hh, i: (b * nq + i, 3 * nh + hh))],
        out_specs=pl.BlockSpec((tq, SB_DIM), lambda b, hh, i: (b * nq + i, hh)),
        scratch_shapes=[pltpu.VMEM((seq // t, SB_DIM, t), BF16), pltpu.VMEM((SB_DIM, tq), F32),
                        pltpu.VMEM((1, tq), F32)],
        compiler_params=_cparams(("arbitrary", "arbitrary", "arbitrary")),
        name="sb_attention",
    )(h, h, h, h)


def _conv_kernel(b_ref, c_ref, u_ref, g_ref, w_ref, bias_ref, o_ref, halo_ref, *, tm, blocks_per_seq):
    i = pl.program_id(0)
    @pl.when(i % blocks_per_seq == 0)
    def _():
        halo_ref[...] = jnp.zeros_like(halo_ref)

    cu = c_ref[...].astype(F32) * u_ref[...].astype(F32)
    prev = halo_ref[...]
    row = lax.broadcasted_iota(jnp.int32, (tm, 1), 0)
    back1 = jnp.where(row == 0, prev[7:8, :], pltpu.roll(cu, 1, axis=0))
    back2 = jnp.where(row == 0, prev[6:7, :],
                      jnp.where(row == 1, prev[7:8, :], pltpu.roll(cu, 2, axis=0)))
    w = w_ref[...]
    y = w[0:1, :] * back2 + w[1:2, :] * back1 + w[2:3, :] * cu + bias_ref[...]
    halo_ref[...] = cu[tm - 8:, :]
    o_ref[...] = (b_ref[...].astype(F32) * y * _silu(g_ref[...].astype(F32))).astype(o_ref.dtype)


def short_conv(h, conv_w, conv_b, seq, *, tm=256):
    n = h.shape[0]
    w = SC_WIDTH
    tm = min(tm, seq)
    kernel = functools.partial(_conv_kernel, tm=tm, blocks_per_seq=seq // tm)
    col = lambda j: pl.BlockSpec((tm, w), lambda i: (i, j))
    return pl.pallas_call(
        kernel,
        out_shape=jax.ShapeDtypeStruct((n, w), BF16),
        grid=(n // tm,),
        in_specs=[col(0), col(1), col(2), col(3),
                  pl.BlockSpec((3, w), lambda i: (0, 0)),
                  pl.BlockSpec((1, w), lambda i: (0, 0))],
        out_specs=pl.BlockSpec((tm, w), lambda i: (i, 0)),
        scratch_shapes=[pltpu.VMEM((8, w), F32)],
        compiler_params=_cparams(("arbitrary",)),
        name="short_conv",
    )(h, h, h, h, conv_w, conv_b.reshape(1, w))


def _rms(x, width):
    return lax.rsqrt(jnp.sum(x * x, axis=-1, keepdims=True) * (1.0 / width) + EPS)


def _mla_prep_kernel(c_ref, wq_ref, wk_ref, wvt_ref, qg_ref, kvg_ref, qn_ref, qr_ref, qrs_ref,
                     kn_ref, kr_ref, krs_ref, cos_ref, sin_ref,
                     q_out, k_out, vt_out, *, scale):
    c = c_ref[...].astype(F32)
    c_q = c[:, :MLA_Q_RANK]
    c_kv = c[:, MLA_Q_RANK:MLA_Q_RANK + MLA_KV_RANK]
    kro = c[:, MLA_Q_RANK + MLA_KV_RANK:MLA_Q_RANK + MLA_KV_RANK + LANES]
    kro_s = c[:, MLA_Q_RANK + MLA_KV_RANK + LANES:]
    cqn = (c_q * _rms(c_q, MLA_Q_RANK) * qg_ref[...]).astype(BF16)
    ckn = (c_kv * _rms(c_kv, MLA_KV_RANK) * kvg_ref[...]).astype(BF16)

    cos = cos_ref[...]
    sin = sin_ref[...]
    lane = lax.broadcasted_iota(jnp.int32, (1, LANES), 1)
    sin_signed = jnp.where(lane < MLA_ROPE // 2, -sin, sin)

    k_rot = (kro * kr_ref[...] * cos + kro_s * krs_ref[...] * sin_signed) * _rms(kro, MLA_ROPE)
    k_rot = k_rot.astype(k_out.dtype)

    vt = lax.dot_general(wvt_ref[...], ckn, (((1,), (1,)), ((), ())), preferred_element_type=F32)
    vt_out[0] = vt.reshape(vt_out.shape[1:]).astype(vt_out.dtype)

    for hh in range(MLA_HEADS):
        q3 = jnp.dot(cqn, wq_ref[:, hh * 3 * LANES:(hh + 1) * 3 * LANES], preferred_element_type=F32)
        qn = q3[:, :LANES]
        qr = q3[:, LANES:2 * LANES]
        qs = q3[:, 2 * LANES:]
        qn = qn * _rms(qn, MLA_NOPE) * qn_ref[...]
        q_rot = (qr * qr_ref[...] * cos + qs * qrs_ref[...] * sin_signed) * _rms(qr, MLA_ROPE)
        q_out[:, hh * MLA_QK_PAD:hh * MLA_QK_PAD + LANES] = (qn * scale).astype(q_out.dtype)
        q_out[:, hh * MLA_QK_PAD + LANES:(hh + 1) * MLA_QK_PAD] = (q_rot * scale).astype(q_out.dtype)

        kn = jnp.dot(ckn, wk_ref[:, hh * LANES:(hh + 1) * LANES], preferred_element_type=F32)
        kn = kn * _rms(kn, MLA_NOPE) * kn_ref[...]
        k_out[:, hh * MLA_QK_PAD:hh * MLA_QK_PAD + LANES] = kn.astype(k_out.dtype)
        k_out[:, hh * MLA_QK_PAD + LANES:(hh + 1) * MLA_QK_PAD] = k_rot


def mla_prep(h, wq, wk, wvt, gains, cos, sin, *, tm):
    n = h.shape[0]
    cw = 1024
    row = lambda width, j=0: pl.BlockSpec((tm, width), lambda i: (i, j))
    full = lambda a: pl.BlockSpec(a.shape, lambda i: (0, 0))
    kernel = functools.partial(_mla_prep_kernel, scale=math.log2(math.e) / math.sqrt(MLA_NOPE + MLA_ROPE))
    return pl.pallas_call(
        kernel,
        out_shape=[jax.ShapeDtypeStruct((n, MLA_HEADS * MLA_QK_PAD), BF16),
                   jax.ShapeDtypeStruct((n, MLA_HEADS * MLA_QK_PAD), BF16),
                   jax.ShapeDtypeStruct((n // tm, MLA_HEADS, MLA_V, tm), BF16)],
        grid=(n // tm,),
        in_specs=[row(cw, 2), full(wq), full(wk), full(wvt)] + [full(a) for a in gains]
                 + [row(LANES), row(LANES)],
        out_specs=[row(MLA_HEADS * MLA_QK_PAD), row(MLA_HEADS * MLA_QK_PAD),
                   pl.BlockSpec((1, MLA_HEADS, MLA_V, tm), lambda i: (i, 0, 0, 0))],
        compiler_params=_cparams(("arbitrary",)),
        name="mla_prep",
    )(h, wq, wk, wvt, *gains, cos, sin)


def _mla_attn_kernel(q_ref, k_ref, vt_ref, g_ref, o_ref, s0_ref, s1_ref, m_ref, l_ref, acc_ref, *, tq, tk):
    qi = pl.program_id(2)
    m_ref[...] = jnp.full_like(m_ref, -jnp.inf)
    l_ref[...] = jnp.zeros_like(l_ref)
    acc_ref[...] = jnp.zeros_like(acc_ref)

    def scores(kb, dst, lo=0):
        ks = pl.multiple_of(kb * tk, tk)
        dst[:, lo:] = lax.dot_general(k_ref[pl.ds(ks, tk), :], q_ref[lo:, :], (((1,), (1,)), ((), ())),
                                      preferred_element_type=F32)

    def consume(src, kb, lo=0, diag=None):
        s = src[:, lo:]
        if diag is not None:
            key = diag + lax.broadcasted_iota(jnp.int32, s.shape, 0)
            qry = lo + lax.broadcasted_iota(jnp.int32, s.shape, 1)
            s = jnp.where(key // CHUNK <= qry // CHUNK, s, -jnp.inf)
        m_old = m_ref[:, lo:]
        m_new = jnp.maximum(m_old, jnp.max(s, axis=0, keepdims=True))
        alpha = jnp.exp2(m_old - m_new)
        p = jnp.exp2(s - m_new)
        l_ref[:, lo:] = alpha * l_ref[:, lo:] + jnp.sum(p, axis=0, keepdims=True)
        acc_ref[:, lo:] = alpha * acc_ref[:, lo:] + jnp.dot(vt_ref[kb, 0], p.astype(BF16),
                                                            preferred_element_type=F32)
        m_ref[:, lo:] = m_new

    def body(j, carry):
        scores(2 * j + 1, s1_ref)
        consume(s0_ref, 2 * j)
        scores(2 * j + 2, s0_ref)
        consume(s1_ref, 2 * j + 1)
        return carry

    scores(0, s0_ref)
    lax.fori_loop(0, qi, body, 0)
    scores(2 * qi + 1, s1_ref, lo=tk)
    consume(s0_ref, 2 * qi, diag=0)
    consume(s1_ref, 2 * qi + 1, lo=tk, diag=tk)
    o = (acc_ref[...] * (1.0 / l_ref[...])).T
    o_ref[...] = (o * _silu(g_ref[...].astype(F32))).astype(o_ref.dtype)


def mla_attention(qf, kf, vt, h, batch, seq, *, tq, tk):
    n = qf.shape[0]
    nq = seq // tq
    nkb = seq // tk
    kernel = functools.partial(_mla_attn_kernel, tq=tq, tk=tk)
    return pl.pallas_call(
        kernel,
        out_shape=jax.ShapeDtypeStruct((n, MLA_HEADS * MLA_V), BF16),
        grid=(batch, MLA_HEADS, nq),
        in_specs=[pl.BlockSpec((tq, MLA_QK_PAD), lambda b, hh, i: (b * nq + i, hh)),
                  pl.BlockSpec((seq, MLA_QK_PAD), lambda b, hh, i: (b, hh)),
                  pl.BlockSpec((nkb, 1, MLA_V, tk), lambda b, hh, i: (b, hh, 0, 0)),
                  pl.BlockSpec((tq, MLA_V), lambda b, hh, i: (b * nq + i, hh))],
        out_specs=pl.BlockSpec((tq, MLA_V), lambda b, hh, i: (b * nq + i, hh)),
        scratch_shapes=[pltpu.VMEM((tk, tq), F32), pltpu.VMEM((tk, tq), F32),
                        pltpu.VMEM((1, tq), F32), pltpu.VMEM((1, tq), F32), pltpu.VMEM((MLA_V, tq), F32)],
        compiler_params=_cparams(("arbitrary", "arbitrary", "arbitrary")),
        name="mla_attention",
    )(qf, kf, vt, h)


def _ret_kernel(q_ref, k_ref, v_ref, g_ref, cos_ref, sin_ref, on_ref, o_ref,
                state_ref, decay_ref, qd_ref, kd_ref, cd_ref, *, c):
    hh = pl.program_id(1)
    ci = pl.program_id(2)

    @pl.when(ci == 0)
    def _():
        state_ref[...] = jnp.zeros_like(state_ref)
        head = jnp.full((1, 1), hh, jnp.int32).astype(F32)
        log_gamma = jnp.log1p(-jnp.exp2(-5.0 - head))
        n_row = lax.broadcasted_iota(jnp.int32, (c, c), 0)
        n_col = lax.broadcasted_iota(jnp.int32, (c, c), 1)
        rel = (n_row - n_col).astype(F32)
        decay_ref[...] = jnp.where(rel >= 0, jnp.exp(log_gamma * jnp.maximum(rel, 0.0)), 0.0)
        n1 = lax.broadcasted_iota(jnp.int32, (c, 1), 0).astype(F32)
        qd_ref[...] = jnp.exp(log_gamma * (n1 + 1.0))
        kd_ref[...] = jnp.exp(log_gamma * (c - 1.0 - n1))
        cd_ref[...] = jnp.exp(log_gamma * c)

    cos = cos_ref[...]
    sin = sin_ref[...]
    half = RET_QK // 2

    def rope(x):
        x1, x2 = x[:, :half], x[:, half:]
        return jnp.concatenate([x1 * cos - x2 * sin, x1 * sin + x2 * cos], axis=1)

    q = rope(q_ref[...].astype(F32))
    k = rope(k_ref[...].astype(F32)) * (RET_QK ** -0.5)
    v = v_ref[...]
    state = state_ref[...]

    intra = lax.dot_general(q.astype(BF16), k.astype(BF16), (((1,), (1,)), ((), ())),
                            preferred_element_type=F32) * decay_ref[...]
    out = (jnp.dot(intra.astype(BF16), v, preferred_element_type=F32)
           + jnp.dot((q * qd_ref[...]).astype(BF16), state.astype(BF16), preferred_element_type=F32))
    kt = (k * kd_ref[...]).T.astype(BF16)
    state_ref[...] = state * cd_ref[...] + jnp.dot(kt, v, preferred_element_type=F32)

    normed = out * _rms(out, RET_V) * on_ref[...]
    o_ref[...] = (_silu(g_ref[...].astype(F32)) * normed).astype(o_ref.dtype)


def retention(h, cos, sin, out_norm, batch, seq, *, c=256):
    n = h.shape[0]
    c = min(c, seq)
    nc = seq // c
    nh = RET_HEADS
    kernel = functools.partial(_ret_kernel, c=c)
    return pl.pallas_call(
        kernel,
        out_shape=jax.ShapeDtypeStruct((n, nh * RET_V), BF16),
        grid=(batch, nh, nc),
        in_specs=[pl.BlockSpec((c, RET_QK), lambda b, hh, i: (b * nc + i, hh)),
                  pl.BlockSpec((c, RET_QK), lambda b, hh, i: (b * nc + i, nh + hh)),
                  pl.BlockSpec((c, RET_V), lambda b, hh, i: (b * nc + i, nh + hh)),
                  pl.BlockSpec((c, RET_V), lambda b, hh, i: (b * nc + i, 2 * nh + hh)),
                  pl.BlockSpec((c, LANES), lambda b, hh, i: (b * nc + i, 0)),
                  pl.BlockSpec((c, LANES), lambda b, hh, i: (b * nc + i, 0)),
                  pl.BlockSpec((1, RET_V), lambda b, hh, i: (0, hh))],
        out_specs=pl.BlockSpec((c, RET_V), lambda b, hh, i: (b * nc + i, hh)),
        scratch_shapes=[pltpu.VMEM((RET_QK, RET_V), F32), pltpu.VMEM((c, c), F32),
                        pltpu.VMEM((c, 1), F32), pltpu.VMEM((c, 1), F32), pltpu.VMEM((1, 1), F32)],
        compiler_params=_cparams(("arbitrary", "arbitrary", "arbitrary")),
        name="retention",
    )(h, h, h, h, cos, sin, out_norm.reshape(1, nh * RET_V))


def _rope_cols(w):
    half = MLA_ROPE // 2
    x1, x2 = w[..., :half], w[..., half:]
    z = jnp.zeros(w.shape[:-1] + (LANES - MLA_ROPE,), w.dtype)
    return jnp.concatenate([x1, x2, z, x2, x1, z], axis=-1)


def _mla_weights(w_in, w_uq, w_ukv):
    q_end = MLA_Q_RANK
    kv_end = q_end + MLA_KV_RANK
    r_end = kv_end + MLA_ROPE
    w_in_x = jnp.concatenate([w_in[:, r_end:], w_in[:, :kv_end], _rope_cols(w_in[:, kv_end:r_end])], axis=1)
    uq = w_uq.reshape(MLA_Q_RANK, MLA_HEADS, MLA_NOPE + MLA_ROPE)
    wq = jnp.concatenate([uq[..., :MLA_NOPE], _rope_cols(uq[..., MLA_NOPE:])], axis=-1)
    wq = wq.reshape(MLA_Q_RANK, MLA_HEADS * 3 * LANES)
    ukv = w_ukv.reshape(MLA_KV_RANK, MLA_HEADS, MLA_NOPE + MLA_V)
    wk = ukv[..., :MLA_NOPE].reshape(MLA_KV_RANK, MLA_HEADS * MLA_NOPE)
    wvt = ukv[..., MLA_NOPE:].reshape(MLA_KV_RANK, MLA_HEADS * MLA_V).T
    return w_in_x.astype(BF16), wq.astype(BF16), wk.astype(BF16), wvt.astype(BF16)


def _rope_gain(g):
    both = _rope_cols(g.reshape(1, MLA_ROPE))
    return both[:, :LANES], both[:, LANES:]


def kernel(x, positions, norm_g, sb_w_in, sb_w_out, sc_w_in, sc_conv_w, sc_conv_b, sc_w_out,
           mla_w_in, mla_q_norm, mla_w_uq, mla_kv_norm, mla_w_ukv, mla_qn_nope, mla_qn_rope,
           mla_kn_nope, mla_kn_rope, mla_w_out, ret_w_in, ret_out_norm, ret_w_out):
    batch, seq, d = x.shape
    n = batch * seq
    depth = norm_g.shape[0]
    xs = x.reshape(n, d)
    pos = positions.reshape(n, 1).astype(F32)
    mla_cos, mla_sin = rope_tables(pos, MLA_ROPE // 2)
    ret_cos, ret_sin = rope_tables(pos, RET_QK // 2)

    for i in range(depth):
        kind, j = i % 4, i // 4
        if kind == 0:
            q_scale = math.log2(math.e) / math.sqrt(SB_DIM)
            w_in = jnp.concatenate([sb_w_in[j][:, :SB_HEADS * SB_DIM] * q_scale,
                                    sb_w_in[j][:, SB_HEADS * SB_DIM:]], axis=1)
            h = norm_matmul(xs, norm_g[i], w_in.astype(BF16))
            z = sb_attention(h, batch, seq)
            w_out = sb_w_out[j]
        elif kind == 1:
            h = norm_matmul(xs, norm_g[i], sc_w_in[j].astype(BF16))
            z = short_conv(h, sc_conv_w[j], sc_conv_b[j], seq)
            w_out = sc_w_out[j]
        elif kind == 2:
            w_in_x, wq, wk, wvt = _mla_weights(mla_w_in[j], mla_w_uq[j], mla_w_ukv[j])
            h = norm_matmul(xs, norm_g[i], w_in_x)
            qr, qrs = _rope_gain(mla_qn_rope[j])
            kr, krs = _rope_gain(mla_kn_rope[j])
            gains = [mla_q_norm[j].reshape(1, -1), mla_kv_norm[j].reshape(1, -1),
                     mla_qn_nope[j].reshape(1, -1), qr, qrs, mla_kn_nope[j].reshape(1, -1), kr, krs]
            tk = min(MLA_TK, seq // 2)
            qf, kf, vt = mla_prep(h, wq, wk, wvt, gains, mla_cos, mla_sin, tm=tk)
            z = mla_attention(qf, kf, vt, h, batch, seq, tq=2 * tk, tk=tk)
            w_out = mla_w_out[j]
        else:
            h = norm_matmul(xs, norm_g[i], ret_w_in[j].astype(BF16))
            z = retention(h, ret_cos, ret_sin, ret_out_norm[j], batch, seq)
            w_out = ret_w_out[j]
        xs = matmul_residual(z, w_out.astype(BF16), xs, tn=1024 if z.shape[1] <= 2048 else 512)
    return xs.reshape(batch, seq, d)
```

```python
import functools
import math

import jax
import jax.numpy as jnp
from jax import lax
from jax.experimental import pallas as pl
from jax.experimental.pallas import tpu as pltpu

F32 = jnp.float32
BF16 = jnp.bfloat16

D_MODEL = 2048
EPS = 1e-6
ROPE_BASE = 10000.0
CHUNK = 64

LANES = 128
SB_HEADS, SB_DIM = 16, 128
SC_WIDTH = 2048
MLA_HEADS, MLA_NOPE, MLA_ROPE, MLA_V = 16, 128, 64, 128
MLA_Q_RANK, MLA_KV_RANK = 512, 256
MLA_QK_PAD = 256
MLA_TK = 512
RET_HEADS, RET_QK, RET_V = 8, 256, 512

VMEM_LIMIT = 56 * 1024 * 1024

SB_T = 256
SB_GROUPS = 4
EXP2_ZERO_BELOW = -150.0


def _cparams(sem):
    return pltpu.CompilerParams(dimension_semantics=sem, vmem_limit_bytes=VMEM_LIMIT)


def _silu(x):
    return x * (1.0 / (1.0 + jnp.exp(-x)))


def _norm_matmul_kernel(x_ref, g_ref, w_ref, o_ref, xn_ref):
    @pl.when(pl.program_id(1) == 0)
    def _():
        x = x_ref[...]
        ms = jnp.mean(x * x, axis=-1, keepdims=True)
        xn_ref[...] = (x * lax.rsqrt(ms + EPS) * g_ref[...]).astype(BF16)

    o_ref[...] = jnp.dot(xn_ref[...], w_ref[...], preferred_element_type=F32).astype(o_ref.dtype)


def norm_matmul(x, g, w, *, tm=1024, tn=1024):
    n, d = x.shape
    dout = w.shape[1]
    tm, tn = min(tm, n), min(tn, dout)
    return pl.pallas_call(
        _norm_matmul_kernel,
        out_shape=jax.ShapeDtypeStruct((n, dout), BF16),
        grid=(n // tm, dout // tn),
        in_specs=[pl.BlockSpec((tm, d), lambda i, j: (i, 0)),
                  pl.BlockSpec((1, d), lambda i, j: (0, 0)),
                  pl.BlockSpec((d, tn), lambda i, j: (0, j))],
        out_specs=pl.BlockSpec((tm, tn), lambda i, j: (i, j)),
        scratch_shapes=[pltpu.VMEM((tm, d), BF16)],
        compiler_params=_cparams(("arbitrary", "arbitrary")),
        name="norm_matmul",
    )(x, g.reshape(1, d), w)


def _matmul_residual_kernel(z_ref, w_ref, x_ref, o_ref):
    o_ref[...] = x_ref[...] + jnp.dot(z_ref[...], w_ref[...], preferred_element_type=F32)


def matmul_residual(z, w, x, *, tm=1024, tn=1024):
    n, k = z.shape
    dout = w.shape[1]
    tm, tn = min(tm, n), min(tn, dout)
    return pl.pallas_call(
        _matmul_residual_kernel,
        out_shape=jax.ShapeDtypeStruct((n, dout), F32),
        grid=(n // tm, dout // tn),
        in_specs=[pl.BlockSpec((tm, k), lambda i, j: (i, 0)),
                  pl.BlockSpec((k, tn), lambda i, j: (0, j)),
                  pl.BlockSpec((tm, tn), lambda i, j: (i, j))],
        out_specs=pl.BlockSpec((tm, tn), lambda i, j: (i, j)),
        compiler_params=_cparams(("arbitrary", "arbitrary")),
        name="matmul_residual",
    )(z, w, x)


def _rope_table_kernel(pos_ref, cos_ref, sin_ref, *, half):
    lane = lax.broadcasted_iota(jnp.int32, (1, LANES), 1) % half
    inv_freq = jnp.exp(lane.astype(F32) * (-math.log(ROPE_BASE) / half))
    ang = pos_ref[...] * inv_freq
    cos_ref[...] = jnp.cos(ang)
    sin_ref[...] = jnp.sin(ang)


def rope_tables(pos, half, *, tm=1024):
    n = pos.shape[0]
    tm = min(tm, n)
    spec = pl.BlockSpec((tm, LANES), lambda i: (i, 0))
    return pl.pallas_call(
        functools.partial(_rope_table_kernel, half=half),
        out_shape=[jax.ShapeDtypeStruct((n, LANES), F32)] * 2,
        grid=(n // tm,),
        in_specs=[pl.BlockSpec((tm, 1), lambda i: (i, 0))],
        out_specs=[spec, spec],
        compiler_params=_cparams(("arbitrary",)),
        name="rope_tables",
    )(pos)


def _sb_kernel(q_ref, k_ref, v_ref, g_ref, o_ref, vt_ref, acc_ref, r_ref, *, t, groups):
    qi = pl.program_id(2)

    @pl.when(qi == 0)
    def _():
        for j in range(vt_ref.shape[0]):
            vt_ref[j] = v_ref[j * t:(j + 1) * t, :].astype(F32).T.astype(BF16)

    acc_ref[...] = jnp.zeros_like(acc_ref)
    tq = groups * t
    key = lax.broadcasted_iota(jnp.int32, (t, t), 0)
    later = (lax.broadcasted_iota(jnp.int32, (t, t), 1) > key).astype(BF16)
    later2 = jnp.concatenate([later, later], axis=1)
    lane = lax.broadcasted_iota(jnp.int32, (1, tq), 1)
    first = groups * qi

    group_lanes = [slice(g * t, (g + 1) * t) for g in range(groups)]
    strict = key < lax.broadcasted_iota(jnp.int32, (t, t), 1)

    def block_ids(d):
        return [jnp.maximum(first + g - d, 0) for g in range(groups)]

    def scan_keys(d, diag):
        zs = [lax.dot_general(k_ref[pl.ds(pl.multiple_of(kb * t, t), t), :], q_ref[lanes, :],
                              (((1,), (1,)), ((), ())), preferred_element_type=F32)
              for kb, lanes in zip(block_ids(d), group_lanes)]
        log_betas, afters, heads = [], [], []
        for z in zs:
            log_beta = jnp.minimum(z, 0.0) - jnp.log2(1.0 + jnp.exp2(-jnp.abs(z)))
            log_1m = log_beta - z
            if diag:
                log_1m = jnp.where(strict, log_1m, 0.0)
            hi = log_1m.astype(BF16)
            lo = (log_1m - hi.astype(F32)).astype(BF16)
            afters.append(jnp.dot(later2, jnp.concatenate([hi, lo], axis=0), preferred_element_type=F32))
            log_betas.append(log_beta)
            heads.append(log_1m[0:1, :])
        return log_betas, afters, heads

    def add_values(d, diag, r, scanned):
        log_betas, afters, heads = scanned
        r = jnp.where(lane // t >= d - first, r, -jnp.inf)
        r_new = []
        for g, (kb, lanes) in enumerate(zip(block_ids(d), group_lanes)):
            a = jnp.exp2(log_betas[g] + afters[g] + r[:, lanes])
            if diag:
                a = jnp.where(strict, a, 0.0)
            acc_ref[:, lanes] += jnp.dot(vt_ref[kb], a.astype(BF16), preferred_element_type=F32)
            r_new.append(r[:, lanes] + afters[g][0:1, :] + heads[g])
        return jnp.concatenate(r_new, axis=1)

    scanned0 = scan_keys(0, True)
    scanned1 = scan_keys(1, False)
    r1 = add_values(0, True, jnp.zeros((1, tq), F32), scanned0)
    r2 = add_values(1, False, r1, scanned1)
    r_ref[...] = r2

    def cond(carry):
        d, rmax = carry
        return jnp.logical_and(d <= first + groups - 1, rmax >= EXP2_ZERO_BELOW)

    def body(carry):
        d, _ = carry
        r_new = add_values(d, False, r_ref[...], scan_keys(d, False))
        r_ref[...] = r_new
        return d + 1, jnp.max(r_new)

    lax.while_loop(cond, body, (jnp.int32(2), jnp.max(r2)))
    o_ref[...] = (acc_ref[...].T * _silu(g_ref[...].astype(F32))).astype(o_ref.dtype)


def sb_attention(h, batch, seq):
    n = h.shape[0]
    t = min(SB_T, seq)
    groups = min(SB_GROUPS, seq // t)
    tq = t * groups
    nq = seq // tq
    nh = SB_HEADS
    kernel = functools.partial(_sb_kernel, t=t, groups=groups)
    return pl.pallas_call(
        kernel,
        out_shape=jax.ShapeDtypeStruct((n, nh * SB_DIM), BF16),
        grid=(batch, nh, nq),
        in_specs=[pl.BlockSpec((tq, SB_DIM), lambda b, hh, i: (b * nq + i, hh)),
                  pl.BlockSpec((seq, SB_DIM), lambda b, hh, i: (b, nh + hh)),
                  pl.BlockSpec((seq, SB_DIM), lambda b, hh, i: (b, 2 * nh + hh)),
                  pl.BlockSpec((tq, SB_DIM), lambda b, hh, i: (b * nq + i, 3 * nh + hh))],
        out_specs=pl.BlockSpec((tq, SB_DIM), lambda b, hh, i: (b * nq + i, hh)),
        scratch_shapes=[pltpu.VMEM((seq // t, SB_DIM, t), BF16), pltpu.VMEM((SB_DIM, tq), F32),
                        pltpu.VMEM((1, tq), F32)],
        compiler_params=_cparams(("arbitrary", "arbitrary", "arbitrary")),
        name="sb_attention",
    )(h, h, h, h)


def _conv_kernel(b_ref, c_ref, u_ref, g_ref, w_ref, bias_ref, o_ref, halo_ref, *, tm, blocks_per_seq):
    i = pl.program_id(0)
    @pl.when(i % blocks_per_seq == 0)
    def _():
        halo_ref[...] = jnp.zeros_like(halo_ref)

    cu = c_ref[...].astype(F32) * u_ref[...].astype(F32)
    prev = halo_ref[...]
    row = lax.broadcasted_iota(jnp.int32, (tm, 1), 0)
    back1 = jnp.where(row == 0, prev[7:8, :], pltpu.roll(cu, 1, axis=0))
    back2 = jnp.where(row == 0, prev[6:7, :],
                      jnp.where(row == 1, prev[7:8, :], pltpu.roll(cu, 2, axis=0)))
    w = w_ref[...]
    y = w[0:1, :] * back2 + w[1:2, :] * back1 + w[2:3, :] * cu + bias_ref[...]
    halo_ref[...] = cu[tm - 8:, :]
    o_ref[...] = (b_ref[...].astype(F32) * y * _silu(g_ref[...].astype(F32))).astype(o_ref.dtype)


def short_conv(h, conv_w, conv_b, seq, *, tm=256):
    n = h.shape[0]
    w = SC_WIDTH
    tm = min(tm, seq)
    kernel = functools.partial(_conv_kernel, tm=tm, blocks_per_seq=seq // tm)
    col = lambda j: pl.BlockSpec((tm, w), lambda i: (i, j))
    return pl.pallas_call(
        kernel,
        out_shape=jax.ShapeDtypeStruct((n, w), BF16),
        grid=(n // tm,),
        in_specs=[col(0), col(1), col(2), col(3),
                  pl.BlockSpec((3, w), lambda i: (0, 0)),
                  pl.BlockSpec((1, w), lambda i: (0, 0))],
        out_specs=pl.BlockSpec((tm, w), lambda i: (i, 0)),
        scratch_shapes=[pltpu.VMEM((8, w), F32)],
        compiler_params=_cparams(("arbitrary",)),
        name="short_conv",
    )(h, h, h, h, conv_w, conv_b.reshape(1, w))


def _rms(x, width):
    return lax.rsqrt(jnp.sum(x * x, axis=-1, keepdims=True) * (1.0 / width) + EPS)


def _mla_prep_kernel(c_ref, wq_ref, wk_ref, wvt_ref, qg_ref, kvg_ref, qn_ref, qr_ref, qrs_ref,
                     kn_ref, kr_ref, krs_ref, cos_ref, sin_ref,
                     q_out, k_out, vt_out, *, scale):
    c = c_ref[...].astype(F32)
    c_q = c[:, :MLA_Q_RANK]
    c_kv = c[:, MLA_Q_RANK:MLA_Q_RANK + MLA_KV_RANK]
    kro = c[:, MLA_Q_RANK + MLA_KV_RANK:MLA_Q_RANK + MLA_KV_RANK + LANES]
    kro_s = c[:, MLA_Q_RANK + MLA_KV_RANK + LANES:]
    cqn = (c_q * _rms(c_q, MLA_Q_RANK) * qg_ref[...]).astype(BF16)
    ckn = (c_kv * _rms(c_kv, MLA_KV_RANK) * kvg_ref[...]).astype(BF16)

    cos = cos_ref[...]
    sin = sin_ref[...]
    lane = lax.broadcasted_iota(jnp.int32, (1, LANES), 1)
    sin_signed = jnp.where(lane < MLA_ROPE // 2, -sin, sin)

    k_rot = (kro * kr_ref[...] * cos + kro_s * krs_ref[...] * sin_signed) * _rms(kro, MLA_ROPE)
    k_rot = k_rot.astype(k_out.dtype)

    vt = lax.dot_general(wvt_ref[...], ckn, (((1,), (1,)), ((), ())), preferred_element_type=F32)
    vt_out[0] = vt.reshape(vt_out.shape[1:]).astype(vt_out.dtype)

    for pair in range(MLA_HEADS // 2):
        q6 = jnp.dot(cqn, wq_ref[:, pair * 6 * LANES:(pair + 1) * 6 * LANES], preferred_element_type=F32)
        kn2 = jnp.dot(ckn, wk_ref[:, pair * 2 * LANES:(pair + 1) * 2 * LANES], preferred_element_type=F32)
        for sub in range(2):
            hh = 2 * pair + sub
            qn = q6[:, sub * 3 * LANES:sub * 3 * LANES + LANES]
            qr = q6[:, sub * 3 * LANES + LANES:sub * 3 * LANES + 2 * LANES]
            qs = q6[:, sub * 3 * LANES + 2 * LANES:(sub + 1) * 3 * LANES]
            qn = qn * _rms(qn, MLA_NOPE) * qn_ref[...]
            q_rot = (qr * qr_ref[...] * cos + qs * qrs_ref[...] * sin_signed) * _rms(qr, MLA_ROPE)
            q_out[:, hh * MLA_QK_PAD:hh * MLA_QK_PAD + LANES] = (qn * scale).astype(q_out.dtype)
            q_out[:, hh * MLA_QK_PAD + LANES:(hh + 1) * MLA_QK_PAD] = (q_rot * scale).astype(q_out.dtype)

            kn = kn2[:, sub * LANES:(sub + 1) * LANES]
            kn = kn * _rms(kn, MLA_NOPE) * kn_ref[...]
            k_out[:, hh * MLA_QK_PAD:hh * MLA_QK_PAD + LANES] = kn.astype(k_out.dtype)
            k_out[:, hh * MLA_QK_PAD + LANES:(hh + 1) * MLA_QK_PAD] = k_rot


def mla_prep(h, wq, wk, wvt, gains, cos, sin, *, tm):
    n = h.shape[0]
    cw = 1024
    row = lambda width, j=0: pl.BlockSpec((tm, width), lambda i: (i, j))
    full = lambda a: pl.BlockSpec(a.shape, lambda i: (0, 0))
    kernel = functools.partial(_mla_prep_kernel, scale=math.log2(math.e) / math.sqrt(MLA_NOPE + MLA_ROPE))
    return pl.pallas_call(
        kernel,
        out_shape=[jax.ShapeDtypeStruct((n, MLA_HEADS * MLA_QK_PAD), BF16),
                   jax.ShapeDtypeStruct((n, MLA_HEADS * MLA_QK_PAD), BF16),
                   jax.ShapeDtypeStruct((n // tm, MLA_HEADS, MLA_V, tm), BF16)],
        grid=(n // tm,),
        in_specs=[row(cw, 2), full(wq), full(wk), full(wvt)] + [full(a) for a in gains]
                 + [row(LANES), row(LANES)],
        out_specs=[row(MLA_HEADS * MLA_QK_PAD), row(MLA_HEADS * MLA_QK_PAD),
                   pl.BlockSpec((1, MLA_HEADS, MLA_V, tm), lambda i: (i, 0, 0, 0))],
        compiler_params=_cparams(("arbitrary",)),
        name="mla_prep",
    )(h, wq, wk, wvt, *gains, cos, sin)


def _mla_attn_kernel(q_ref, k_ref, vt_ref, g_ref, o_ref, s0_ref, s1_ref, m_ref, l_ref, acc_ref, *, tq, tk):
    qi = pl.program_id(2)
    m_ref[...] = jnp.full_like(m_ref, -jnp.inf)
    l_ref[...] = jnp.zeros_like(l_ref)
    acc_ref[...] = jnp.zeros_like(acc_ref)

    def scores(kb, dst, lo=0):
        ks = pl.multiple_of(kb * tk, tk)
        dst[:, lo:] = lax.dot_general(k_ref[pl.ds(ks, tk), :], q_ref[lo:, :], (((1,), (1,)), ((), ())),
                                      preferred_element_type=F32)

    def consume(src, kb, lo=0, diag=None):
        s = src[:, lo:]
        if diag is not None:
            key = diag + lax.broadcasted_iota(jnp.int32, s.shape, 0)
            qry = lo + lax.broadcasted_iota(jnp.int32, s.shape, 1)
            s = jnp.where(key // CHUNK <= qry // CHUNK, s, -jnp.inf)
        m_old = m_ref[:, lo:]
        m_new = jnp.maximum(m_old, jnp.max(s, axis=0, keepdims=True))
        alpha = jnp.exp2(m_old - m_new)
        p = jnp.exp2(s - m_new)
        l_ref[:, lo:] = alpha * l_ref[:, lo:] + jnp.sum(p, axis=0, keepdims=True)
        acc_ref[:, lo:] = alpha * acc_ref[:, lo:] + jnp.dot(vt_ref[kb, 0], p.astype(BF16),
                                                            preferred_element_type=F32)
        m_ref[:, lo:] = m_new

    def body(j, carry):
        scores(2 * j + 1, s1_ref)
        consume(s0_ref, 2 * j)
        scores(2 * j + 2, s0_ref)
        consume(s1_ref, 2 * j + 1)
        return carry

    scores(0, s0_ref)
    lax.fori_loop(0, qi, body, 0)
    scores(2 * qi + 1, s1_ref, lo=tk)
    consume(s0_ref, 2 * qi, diag=0)
    consume(s1_ref, 2 * qi + 1, lo=tk, diag=tk)
    o = (acc_ref[...] * (1.0 / l_ref[...])).T
    o_ref[...] = (o * _silu(g_ref[...].astype(F32))).astype(o_ref.dtype)


def mla_attention(qf, kf, vt, h, batch, seq, *, tq, tk):
    n = qf.shape[0]
    nq = seq // tq
    nkb = seq // tk
    kernel = functools.partial(_mla_attn_kernel, tq=tq, tk=tk)
    return pl.pallas_call(
        kernel,
        out_shape=jax.ShapeDtypeStruct((n, MLA_HEADS * MLA_V), BF16),
        grid=(batch, MLA_HEADS, nq),
        in_specs=[pl.BlockSpec((tq, MLA_QK_PAD), lambda b, hh, i: (b * nq + i, hh)),
                  pl.BlockSpec((seq, MLA_QK_PAD), lambda b, hh, i: (b, hh)),
                  pl.BlockSpec((nkb, 1, MLA_V, tk), lambda b, hh, i: (b, hh, 0, 0)),
                  pl.BlockSpec((tq, MLA_V), lambda b, hh, i: (b * nq + i, hh))],
        out_specs=pl.BlockSpec((tq, MLA_V), lambda b, hh, i: (b * nq + i, hh)),
        scratch_shapes=[pltpu.VMEM((tk, tq), F32), pltpu.VMEM((tk, tq), F32),
                        pltpu.VMEM((1, tq), F32), pltpu.VMEM((1, tq), F32), pltpu.VMEM((MLA_V, tq), F32)],
        compiler_params=_cparams(("arbitrary", "arbitrary", "arbitrary")),
        name="mla_attention",
    )(qf, kf, vt, h)


def _ret_kernel(q_ref, k_ref, v_ref, g_ref, cos_ref, sin_ref, on_ref, o_ref,
                state_ref, decay_ref, qd_ref, kd_ref, cd_ref, *, c):
    hh = pl.program_id(1)
    ci = pl.program_id(2)

    @pl.when(ci == 0)
    def _():
        state_ref[...] = jnp.zeros_like(state_ref)
        head = jnp.full((1, 1), hh, jnp.int32).astype(F32)
        log_gamma = jnp.log1p(-jnp.exp2(-5.0 - head))
        n_row = lax.broadcasted_iota(jnp.int32, (c, c), 0)
        n_col = lax.broadcasted_iota(jnp.int32, (c, c), 1)
        rel = (n_row - n_col).astype(F32)
        decay_ref[...] = jnp.where(rel >= 0, jnp.exp(log_gamma * jnp.maximum(rel, 0.0)), 0.0)
        n1 = lax.broadcasted_iota(jnp.int32, (c, 1), 0).astype(F32)
        qd_ref[...] = jnp.exp(log_gamma * (n1 + 1.0))
        kd_ref[...] = jnp.exp(log_gamma * (c - 1.0 - n1))
        cd_ref[...] = jnp.exp(log_gamma * c)

    cos = cos_ref[...]
    sin = sin_ref[...]
    half = RET_QK // 2

    def rope(x):
        x1, x2 = x[:, :half], x[:, half:]
        return jnp.concatenate([x1 * cos - x2 * sin, x1 * sin + x2 * cos], axis=1)

    q = rope(q_ref[...].astype(F32))
    k = rope(k_ref[...].astype(F32)) * (RET_QK ** -0.5)
    v = v_ref[...]
    state = state_ref[...]

    intra = lax.dot_general(q.astype(BF16), k.astype(BF16), (((1,), (1,)), ((), ())),
                            preferred_element_type=F32) * decay_ref[...]
    out = (jnp.dot(intra.astype(BF16), v, preferred_element_type=F32)
           + jnp.dot((q * qd_ref[...]).astype(BF16), state.astype(BF16), preferred_element_type=F32))
    kt = (k * kd_ref[...]).T.astype(BF16)
    state_ref[...] = state * cd_ref[...] + jnp.dot(kt, v, preferred_element_type=F32)

    normed = out * _rms(out, RET_V) * on_ref[...]
    o_ref[...] = (_silu(g_ref[...].astype(F32)) * normed).astype(o_ref.dtype)


def retention(h, cos, sin, out_norm, batch, seq, *, c=256):
    n = h.shape[0]
    c = min(c, seq)
    nc = seq // c
    nh = RET_HEADS
    kernel = functools.partial(_ret_kernel, c=c)
    return pl.pallas_call(
        kernel,
        out_shape=jax.ShapeDtypeStruct((n, nh * RET_V), BF16),
        grid=(batch, nh, nc),
        in_specs=[pl.BlockSpec((c, RET_QK), lambda b, hh, i: (b * nc + i, hh)),
                  pl.BlockSpec((c, RET_QK), lambda b, hh, i: (b * nc + i, nh + hh)),
                  pl.BlockSpec((c, RET_V), lambda b, hh, i: (b * nc + i, nh + hh)),
                  pl.BlockSpec((c, RET_V), lambda b, hh, i: (b * nc + i, 2 * nh + hh)),
                  pl.BlockSpec((c, LANES), lambda b, hh, i: (b * nc + i, 0)),
                  pl.BlockSpec((c, LANES), lambda b, hh, i: (b * nc + i, 0)),
                  pl.BlockSpec((1, RET_V), lambda b, hh, i: (0, hh))],
        out_specs=pl.BlockSpec((c, RET_V), lambda b, hh, i: (b * nc + i, hh)),
        scratch_shapes=[pltpu.VMEM((RET_QK, RET_V), F32), pltpu.VMEM((c, c), F32),
                        pltpu.VMEM((c, 1), F32), pltpu.VMEM((c, 1), F32), pltpu.VMEM((1, 1), F32)],
        compiler_params=_cparams(("arbitrary", "arbitrary", "arbitrary")),
        name="retention",
    )(h, h, h, h, cos, sin, out_norm.reshape(1, nh * RET_V))


def _rope_cols(w):
    half = MLA_ROPE // 2
    x1, x2 = w[..., :half], w[..., half:]
    z = jnp.zeros(w.shape[:-1] + (LANES - MLA_ROPE,), w.dtype)
    return jnp.concatenate([x1, x2, z, x2, x1, z], axis=-1)


def _mla_weights(w_in, w_uq, w_ukv):
    w_in, w_uq, w_ukv = w_in.astype(BF16), w_uq.astype(BF16), w_ukv.astype(BF16)
    q_end = MLA_Q_RANK
    kv_end = q_end + MLA_KV_RANK
    r_end = kv_end + MLA_ROPE
    w_in_x = jnp.concatenate([w_in[:, r_end:], w_in[:, :kv_end], _rope_cols(w_in[:, kv_end:r_end])], axis=1)
    uq = w_uq.reshape(MLA_Q_RANK, MLA_HEADS, MLA_NOPE + MLA_ROPE)
    wq = jnp.concatenate([uq[..., :MLA_NOPE], _rope_cols(uq[..., MLA_NOPE:])], axis=-1)
    wq = wq.reshape(MLA_Q_RANK, MLA_HEADS * 3 * LANES)
    ukv = w_ukv.reshape(MLA_KV_RANK, MLA_HEADS, MLA_NOPE + MLA_V)
    wk = ukv[..., :MLA_NOPE].reshape(MLA_KV_RANK, MLA_HEADS * MLA_NOPE)
    wvt = ukv[..., MLA_NOPE:].reshape(MLA_KV_RANK, MLA_HEADS * MLA_V).T
    return w_in_x, wq, wk, wvt


def _rope_gain(g):
    both = _rope_cols(g.reshape(1, MLA_ROPE))
    return both[:, :LANES], both[:, LANES:]


def kernel(x, positions, norm_g, sb_w_in, sb_w_out, sc_w_in, sc_conv_w, sc_conv_b, sc_w_out,
           mla_w_in, mla_q_norm, mla_w_uq, mla_kv_norm, mla_w_ukv, mla_qn_nope, mla_qn_rope,
           mla_kn_nope, mla_kn_rope, mla_w_out, ret_w_in, ret_out_norm, ret_w_out):
    batch, seq, d = x.shape
    n = batch * seq
    depth = norm_g.shape[0]
    xs = x.reshape(n, d)
    pos = positions.reshape(n, 1).astype(F32)
    mla_cos, mla_sin = rope_tables(pos, MLA_ROPE // 2)
    ret_cos, ret_sin = rope_tables(pos, RET_QK // 2)

    for i in range(depth):
        kind, j = i % 4, i // 4
        if kind == 0:
            q_scale = math.log2(math.e) / math.sqrt(SB_DIM)
            col = lax.broadcasted_iota(jnp.int32, (1, sb_w_in.shape[-1]), 1)
            w_in = sb_w_in[j] * jnp.where(col < SB_HEADS * SB_DIM, q_scale, 1.0)
            h = norm_matmul(xs, norm_g[i], w_in.astype(BF16))
            z = sb_attention(h, batch, seq)
            w_out = sb_w_out[j]
        elif kind == 1:
            h = norm_matmul(xs, norm_g[i], sc_w_in[j].astype(BF16))
            z = short_conv(h, sc_conv_w[j], sc_conv_b[j], seq)
            w_out = sc_w_out[j]
        elif kind == 2:
            w_in_x, wq, wk, wvt = _mla_weights(mla_w_in[j], mla_w_uq[j], mla_w_ukv[j])
            h = norm_matmul(xs, norm_g[i], w_in_x)
            qr, qrs = _rope_gain(mla_qn_rope[j])
            kr, krs = _rope_gain(mla_kn_rope[j])
            gains = [mla_q_norm[j].reshape(1, -1), mla_kv_norm[j].reshape(1, -1),
                     mla_qn_nope[j].reshape(1, -1), qr, qrs, mla_kn_nope[j].reshape(1, -1), kr, krs]
            tk = min(MLA_TK, seq // 2)
            qf, kf, vt = mla_prep(h, wq, wk, wvt, gains, mla_cos, mla_sin, tm=tk)
            z = mla_attention(qf, kf, vt, h, batch, seq, tq=2 * tk, tk=tk)
            w_out = mla_w_out[j]
        else:
            h = norm_matmul(xs, norm_g[i], ret_w_in[j].astype(BF16))
            z = retention(h, ret_cos, ret_sin, ret_out_norm[j], batch, seq)
            w_out = ret_w_out[j]
        xs = matmul_residual(z, w_out.astype(BF16), xs, tn=1024 if z.shape[1] <= 2048 else 512)
    return xs.reshape(batch, seq, d)
```

```python
import functools
import math

import jax
import jax.numpy as jnp
from jax import lax
from jax.experimental import pallas as pl
from jax.experimental.pallas import tpu as pltpu

F32 = jnp.float32
BF16 = jnp.bfloat16

D_MODEL = 2048
EPS = 1e-6
ROPE_BASE = 10000.0
CHUNK = 64

LANES = 128
SB_HEADS, SB_DIM = 16, 128
SC_WIDTH = 2048
MLA_HEADS, MLA_NOPE, MLA_ROPE, MLA_V = 16, 128, 64, 128
MLA_Q_RANK, MLA_KV_RANK = 512, 256
MLA_QK_PAD = 256
MLA_TK = 512
RET_HEADS, RET_QK, RET_V = 8, 256, 512

VMEM_LIMIT = 56 * 1024 * 1024

SB_T = 256
SB_GROUPS = 4
EXP2_ZERO_BELOW = -150.0


def _cparams(sem):
    return pltpu.CompilerParams(dimension_semantics=sem, vmem_limit_bytes=VMEM_LIMIT)


def _silu(x):
    return x * (1.0 / (1.0 + jnp.exp(-x)))


def _norm_matmul_kernel(x_ref, g_ref, w_ref, o_ref, xn_ref):
    @pl.when(pl.program_id(1) == 0)
    def _():
        x = x_ref[...]
        ms = jnp.mean(x * x, axis=-1, keepdims=True)
        xn_ref[...] = (x * lax.rsqrt(ms + EPS) * g_ref[...]).astype(BF16)

    o_ref[...] = jnp.dot(xn_ref[...], w_ref[...], preferred_element_type=F32).astype(o_ref.dtype)


def norm_matmul(x, g, w, *, tm=1024, tn=1024):
    n, d = x.shape
    dout = w.shape[1]
    tm, tn = min(tm, n), min(tn, dout)
    return pl.pallas_call(
        _norm_matmul_kernel,
        out_shape=jax.ShapeDtypeStruct((n, dout), BF16),
        grid=(n // tm, dout // tn),
        in_specs=[pl.BlockSpec((tm, d), lambda i, j: (i, 0)),
                  pl.BlockSpec((1, d), lambda i, j: (0, 0)),
                  pl.BlockSpec((d, tn), lambda i, j: (0, j))],
        out_specs=pl.BlockSpec((tm, tn), lambda i, j: (i, j)),
        scratch_shapes=[pltpu.VMEM((tm, d), BF16)],
        compiler_params=_cparams(("arbitrary", "arbitrary")),
        name="norm_matmul",
    )(x, g.reshape(1, d), w)


def _matmul_residual_kernel(z_ref, w_ref, x_ref, o_ref):
    o_ref[...] = x_ref[...] + jnp.dot(z_ref[...], w_ref[...], preferred_element_type=F32)


def matmul_residual(z, w, x, *, tm=512):
    n, k = z.shape
    dout = w.shape[1]
    tm = min(tm, n)
    return pl.pallas_call(
        _matmul_residual_kernel,
        out_shape=jax.ShapeDtypeStruct((n, dout), F32),
        grid=(n // tm,),
        in_specs=[pl.BlockSpec((tm, k), lambda i: (i, 0)),
                  pl.BlockSpec((k, dout), lambda i: (0, 0), pipeline_mode=pl.Buffered(1)),
                  pl.BlockSpec((tm, dout), lambda i: (i, 0))],
        out_specs=pl.BlockSpec((tm, dout), lambda i: (i, 0)),
        compiler_params=_cparams(("arbitrary",)),
        name="matmul_residual",
    )(z, w, x)


def _rope_table_kernel(pos_ref, cos_ref, sin_ref, *, half):
    lane = lax.broadcasted_iota(jnp.int32, (1, LANES), 1) % half
    inv_freq = jnp.exp(lane.astype(F32) * (-math.log(ROPE_BASE) / half))
    ang = pos_ref[...] * inv_freq
    cos_ref[...] = jnp.cos(ang)
    sin_ref[...] = jnp.sin(ang)


def rope_tables(pos, half, *, tm=1024):
    n = pos.shape[0]
    tm = min(tm, n)
    spec = pl.BlockSpec((tm, LANES), lambda i: (i, 0))
    return pl.pallas_call(
        functools.partial(_rope_table_kernel, half=half),
        out_shape=[jax.ShapeDtypeStruct((n, LANES), F32)] * 2,
        grid=(n // tm,),
        in_specs=[pl.BlockSpec((tm, 1), lambda i: (i, 0))],
        out_specs=[spec, spec],
        compiler_params=_cparams(("arbitrary",)),
        name="rope_tables",
    )(pos)


def _sb_kernel(q_ref, k_ref, v_ref, g_ref, o_ref, vt_ref, acc_ref, r_ref, *, t, groups):
    qi = pl.program_id(2)

    @pl.when(qi == 0)
    def _():
        for j in range(vt_ref.shape[0]):
            vt_ref[j] = v_ref[j * t:(j + 1) * t, :].astype(F32).T.astype(BF16)

    acc_ref[...] = jnp.zeros_like(acc_ref)
    tq = groups * t
    key = lax.broadcasted_iota(jnp.int32, (t, t), 0)
    from_here = (lax.broadcasted_iota(jnp.int32, (t, t), 1) >= key).astype(BF16)
    from_here2 = jnp.concatenate([from_here, from_here], axis=1)
    lane = lax.broadcasted_iota(jnp.int32, (1, tq), 1)
    first = groups * qi

    group_lanes = [slice(g * t, (g + 1) * t) for g in range(groups)]
    strict = key < lax.broadcasted_iota(jnp.int32, (t, t), 1)

    def block_ids(d):
        return [jnp.maximum(first + g - d, 0) for g in range(groups)]

    def scan_keys(d, diag):
        zs = [lax.dot_general(k_ref[pl.ds(pl.multiple_of(kb * t, t), t), :], q_ref[lanes, :],
                              (((1,), (1,)), ((), ())), preferred_element_type=F32)
              for kb, lanes in zip(block_ids(d), group_lanes)]
        tails = []
        for z in zs:
            sp = jnp.maximum(z, 0.0) + jnp.log2(1.0 + jnp.exp2(-jnp.abs(z)))
            if diag:
                sp = jnp.where(strict, sp, 0.0)
            hi = sp.astype(BF16)
            lo = (sp - hi.astype(F32)).astype(BF16)
            tails.append(jnp.dot(from_here2, jnp.concatenate([hi, lo], axis=0), preferred_element_type=F32))
        return zs, tails

    def add_values(d, diag, r, scanned):
        zs, tails = scanned
        r = jnp.where(lane // t >= d - first, r, -jnp.inf)
        r_new = []
        for g, (kb, lanes) in enumerate(zip(block_ids(d), group_lanes)):
            a = jnp.exp2(zs[g] + r[:, lanes] - tails[g])
            if diag:
                a = jnp.where(strict, a, 0.0)
            acc_ref[:, lanes] += jnp.dot(vt_ref[kb], a.astype(BF16), preferred_element_type=F32)
            r_new.append(r[:, lanes] - tails[g][0:1, :])
        return jnp.concatenate(r_new, axis=1)

    scanned0 = scan_keys(0, True)
    scanned1 = scan_keys(1, False)
    r1 = add_values(0, True, jnp.zeros((1, tq), F32), scanned0)
    r2 = add_values(1, False, r1, scanned1)
    r_ref[...] = r2

    def cond(carry):
        d, rmax = carry
        return jnp.logical_and(d <= first + groups - 1, rmax >= EXP2_ZERO_BELOW)

    def body(carry):
        d, _ = carry
        r_new = add_values(d, False, r_ref[...], scan_keys(d, False))
        r_ref[...] = r_new
        return d + 1, jnp.max(r_new)

    lax.while_loop(cond, body, (jnp.int32(2), jnp.max(r2)))
    o_ref[...] = (acc_ref[...].T * _silu(g_ref[...].astype(F32))).astype(o_ref.dtype)


def sb_attention(h, batch, seq):
    n = h.shape[0]
    t = min(SB_T, seq)
    groups = min(SB_GROUPS, seq // t)
    tq = t * groups
    nq = seq // tq
    nh = SB_HEADS
    kernel = functools.partial(_sb_kernel, t=t, groups=groups)
    return pl.pallas_call(
        kernel,
        out_shape=jax.ShapeDtypeStruct((n, nh * SB_DIM), BF16),
        grid=(batch, nh, nq),
        in_specs=[pl.BlockSpec((tq, SB_DIM), lambda b, hh, i: (b * nq + i, hh)),
                  pl.BlockSpec((seq, SB_DIM), lambda b, hh, i: (b, nh + hh)),
                  pl.BlockSpec((seq, SB_DIM), lambda b, hh, i: (b, 2 * nh + hh)),
                  pl.BlockSpec((tq, SB_DIM), lambda b, hh, i: (b * nq + i, 3 * nh + hh))],
        out_specs=pl.BlockSpec((tq, SB_DIM), lambda b, hh, i: (b * nq + i, hh)),
        scratch_shapes=[pltpu.VMEM((seq // t, SB_DIM, t), BF16), pltpu.VMEM((SB_DIM, tq), F32),
                        pltpu.VMEM((1, tq), F32)],
        compiler_params=_cparams(("arbitrary", "arbitrary", "arbitrary")),
        name="sb_attention",
    )(h, h, h, h)


def _conv_kernel(b_ref, c_ref, u_ref, g_ref, w_ref, bias_ref, o_ref, halo_ref, *, tm, blocks_per_seq):
    i = pl.program_id(0)
    @pl.when(i % blocks_per_seq == 0)
    def _():
        halo_ref[...] = jnp.zeros_like(halo_ref)

    cu = c_ref[...].astype(F32) * u_ref[...].astype(F32)
    prev = halo_ref[...]
    row = lax.broadcasted_iota(jnp.int32, (tm, 1), 0)
    back1 = jnp.where(row == 0, prev[7:8, :], pltpu.roll(cu, 1, axis=0))
    back2 = jnp.where(row == 0, prev[6:7, :],
                      jnp.where(row == 1, prev[7:8, :], pltpu.roll(cu, 2, axis=0)))
    w = w_ref[...]
    y = w[0:1, :] * back2 + w[1:2, :] * back1 + w[2:3, :] * cu + bias_ref[...]
    halo_ref[...] = cu[tm - 8:, :]
    o_ref[...] = (b_ref[...].astype(F32) * y * _silu(g_ref[...].astype(F32))).astype(o_ref.dtype)


def short_conv(h, conv_w, conv_b, seq, *, tm=512):
    n = h.shape[0]
    w = SC_WIDTH
    tm = min(tm, seq)
    kernel = functools.partial(_conv_kernel, tm=tm, blocks_per_seq=seq // tm)
    col = lambda j: pl.BlockSpec((tm, w), lambda i: (i, j))
    return pl.pallas_call(
        kernel,
        out_shape=jax.ShapeDtypeStruct((n, w), BF16),
        grid=(n // tm,),
        in_specs=[col(0), col(1), col(2), col(3),
                  pl.BlockSpec((3, w), lambda i: (0, 0)),
                  pl.BlockSpec((1, w), lambda i: (0, 0))],
        out_specs=pl.BlockSpec((tm, w), lambda i: (i, 0)),
        scratch_shapes=[pltpu.VMEM((8, w), F32)],
        compiler_params=_cparams(("arbitrary",)),
        name="short_conv",
    )(h, h, h, h, conv_w, conv_b.reshape(1, w))


def _rms(x, width):
    return lax.rsqrt(jnp.sum(x * x, axis=-1, keepdims=True) * (1.0 / width) + EPS)


def _mla_prep_kernel(c_ref, wq_ref, wk_ref, wvt_ref, qg_ref, kvg_ref, qn_ref, qr_ref, qrs_ref,
                     kn_ref, kr_ref, krs_ref, cos_ref, sin_ref,
                     q_out, k_out, vt_out, *, scale):
    c = c_ref[...].astype(F32)
    c_q = c[:, :MLA_Q_RANK]
    c_kv = c[:, MLA_Q_RANK:MLA_Q_RANK + MLA_KV_RANK]
    kro = c[:, MLA_Q_RANK + MLA_KV_RANK:MLA_Q_RANK + MLA_KV_RANK + LANES]
    kro_s = c[:, MLA_Q_RANK + MLA_KV_RANK + LANES:]
    cqn = (c_q * _rms(c_q, MLA_Q_RANK) * qg_ref[...]).astype(BF16)
    ckn = (c_kv * _rms(c_kv, MLA_KV_RANK) * kvg_ref[...]).astype(BF16)

    cos = cos_ref[...]
    sin = sin_ref[...]
    lane = lax.broadcasted_iota(jnp.int32, (1, LANES), 1)
    sin_signed = jnp.where(lane < MLA_ROPE // 2, -sin, sin)

    k_rot = (kro * kr_ref[...] * cos + kro_s * krs_ref[...] * sin_signed) * _rms(kro, MLA_ROPE)
    k_rot = k_rot.astype(k_out.dtype)

    vt = lax.dot_general(wvt_ref[...], ckn, (((1,), (1,)), ((), ())), preferred_element_type=F32)
    vt_out[0] = vt.reshape(vt_out.shape[1:]).astype(vt_out.dtype)

    for pair in range(MLA_HEADS // 2):
        q6 = jnp.dot(cqn, wq_ref[:, pair * 6 * LANES:(pair + 1) * 6 * LANES], preferred_element_type=F32)
        kn2 = jnp.dot(ckn, wk_ref[:, pair * 2 * LANES:(pair + 1) * 2 * LANES], preferred_element_type=F32)
        for sub in range(2):
            hh = 2 * pair + sub
            qn = q6[:, sub * 3 * LANES:sub * 3 * LANES + LANES]
            qr = q6[:, sub * 3 * LANES + LANES:sub * 3 * LANES + 2 * LANES]
            qs = q6[:, sub * 3 * LANES + 2 * LANES:(sub + 1) * 3 * LANES]
            qn = qn * _rms(qn, MLA_NOPE) * qn_ref[...]
            q_rot = (qr * qr_ref[...] * cos + qs * qrs_ref[...] * sin_signed) * _rms(qr, MLA_ROPE)
            q_out[:, hh * MLA_QK_PAD:hh * MLA_QK_PAD + LANES] = (qn * scale).astype(q_out.dtype)
            q_out[:, hh * MLA_QK_PAD + LANES:(hh + 1) * MLA_QK_PAD] = (q_rot * scale).astype(q_out.dtype)

            kn = kn2[:, sub * LANES:(sub + 1) * LANES]
            kn = kn * _rms(kn, MLA_NOPE) * kn_ref[...]
            k_out[:, hh * MLA_QK_PAD:hh * MLA_QK_PAD + LANES] = kn.astype(k_out.dtype)
            k_out[:, hh * MLA_QK_PAD + LANES:(hh + 1) * MLA_QK_PAD] = k_rot


def mla_prep(h, wq, wk, wvt, gains, cos, sin, *, tm):
    n = h.shape[0]
    cw = 1024
    row = lambda width, j=0: pl.BlockSpec((tm, width), lambda i: (i, j))
    full = lambda a: pl.BlockSpec(a.shape, lambda i: (0, 0))
    kernel = functools.partial(_mla_prep_kernel, scale=math.log2(math.e) / math.sqrt(MLA_NOPE + MLA_ROPE))
    return pl.pallas_call(
        kernel,
        out_shape=[jax.ShapeDtypeStruct((n, MLA_HEADS * MLA_QK_PAD), BF16),
                   jax.ShapeDtypeStruct((n, MLA_HEADS * MLA_QK_PAD), BF16),
                   jax.ShapeDtypeStruct((n // tm, MLA_HEADS, MLA_V, tm), BF16)],
        grid=(n // tm,),
        in_specs=[row(cw, 2), full(wq), full(wk), full(wvt)] + [full(a) for a in gains]
                 + [row(LANES), row(LANES)],
        out_specs=[row(MLA_HEADS * MLA_QK_PAD), row(MLA_HEADS * MLA_QK_PAD),
                   pl.BlockSpec((1, MLA_HEADS, MLA_V, tm), lambda i: (i, 0, 0, 0))],
        compiler_params=_cparams(("arbitrary",)),
        name="mla_prep",
    )(h, wq, wk, wvt, *gains, cos, sin)


def _mla_attn_kernel(q_ref, k_ref, vt_ref, g_ref, o_ref, s0_ref, s1_ref, m_ref, l_ref, acc_ref, *, tq, tk):
    qi = pl.program_id(2)
    m_ref[...] = jnp.full_like(m_ref, -jnp.inf)
    l_ref[...] = jnp.zeros_like(l_ref)
    acc_ref[...] = jnp.zeros_like(acc_ref)

    def scores(kb, dst, lo=0):
        ks = pl.multiple_of(kb * tk, tk)
        dst[:, lo:] = lax.dot_general(k_ref[pl.ds(ks, tk), :], q_ref[lo:, :], (((1,), (1,)), ((), ())),
                                      preferred_element_type=F32)

    def consume(src, kb, lo=0, diag=None):
        s = src[:, lo:]
        if diag is not None:
            key = diag + lax.broadcasted_iota(jnp.int32, s.shape, 0)
            qry = lo + lax.broadcasted_iota(jnp.int32, s.shape, 1)
            s = jnp.where(key // CHUNK <= qry // CHUNK, s, -jnp.inf)
        m_old = m_ref[:, lo:]
        m_new = jnp.maximum(m_old, jnp.max(s, axis=0, keepdims=True))
        alpha = jnp.exp2(m_old - m_new)
        p = jnp.exp2(s - m_new)
        l_ref[:, lo:] = alpha * l_ref[:, lo:] + jnp.sum(p, axis=0, keepdims=True)
        acc_ref[:, lo:] = alpha * acc_ref[:, lo:] + jnp.dot(vt_ref[kb, 0], p.astype(BF16),
                                                            preferred_element_type=F32)
        m_ref[:, lo:] = m_new

    def body(j, carry):
        scores(2 * j + 1, s1_ref)
        consume(s0_ref, 2 * j)
        scores(2 * j + 2, s0_ref)
        consume(s1_ref, 2 * j + 1)
        return carry

    scores(0, s0_ref)
    lax.fori_loop(0, qi, body, 0)
    scores(2 * qi + 1, s1_ref, lo=tk)
    consume(s0_ref, 2 * qi, diag=0)
    consume(s1_ref, 2 * qi + 1, lo=tk, diag=tk)
    o = (acc_ref[...] * (1.0 / l_ref[...])).T
    o_ref[...] = (o * _silu(g_ref[...].astype(F32))).astype(o_ref.dtype)


def mla_attention(qf, kf, vt, h, batch, seq, *, tq, tk):
    n = qf.shape[0]
    nq = seq // tq
    nkb = seq // tk
    kernel = functools.partial(_mla_attn_kernel, tq=tq, tk=tk)
    return pl.pallas_call(
        kernel,
        out_shape=jax.ShapeDtypeStruct((n, MLA_HEADS * MLA_V), BF16),
        grid=(batch, MLA_HEADS, nq),
        in_specs=[pl.BlockSpec((tq, MLA_QK_PAD), lambda b, hh, i: (b * nq + i, hh)),
                  pl.BlockSpec((seq, MLA_QK_PAD), lambda b, hh, i: (b, hh)),
                  pl.BlockSpec((nkb, 1, MLA_V, tk), lambda b, hh, i: (b, hh, 0, 0)),
                  pl.BlockSpec((tq, MLA_V), lambda b, hh, i: (b * nq + i, hh))],
        out_specs=pl.BlockSpec((tq, MLA_V), lambda b, hh, i: (b * nq + i, hh)),
        scratch_shapes=[pltpu.VMEM((tk, tq), F32), pltpu.VMEM((tk, tq), F32),
                        pltpu.VMEM((1, tq), F32), pltpu.VMEM((1, tq), F32), pltpu.VMEM((MLA_V, tq), F32)],
        compiler_params=_cparams(("arbitrary", "arbitrary", "arbitrary")),
        name="mla_attention",
    )(qf, kf, vt, h)


def _ret_kernel(q_ref, k_ref, v_ref, g_ref, cos_ref, sin_ref, on_ref, o_ref,
                state_ref, decay_ref, qd_ref, kd_ref, cd_ref, *, c):
    hh = pl.program_id(1)
    ci = pl.program_id(2)

    @pl.when(ci == 0)
    def _():
        state_ref[...] = jnp.zeros_like(state_ref)
        head = jnp.full((1, 1), hh, jnp.int32).astype(F32)
        log_gamma = jnp.log1p(-jnp.exp2(-5.0 - head))
        n_row = lax.broadcasted_iota(jnp.int32, (c, c), 0)
        n_col = lax.broadcasted_iota(jnp.int32, (c, c), 1)
        rel = (n_row - n_col).astype(F32)
        decay_ref[...] = jnp.where(rel >= 0, jnp.exp(log_gamma * jnp.maximum(rel, 0.0)), 0.0)
        n1 = lax.broadcasted_iota(jnp.int32, (c, 1), 0).astype(F32)
        qd_ref[...] = jnp.exp(log_gamma * (n1 + 1.0))
        kd_ref[...] = jnp.exp(log_gamma * (c - 1.0 - n1))
        cd_ref[...] = jnp.exp(log_gamma * c)

    cos = cos_ref[...]
    sin = sin_ref[...]
    half = RET_QK // 2

    def rope(x):
        x1, x2 = x[:, :half], x[:, half:]
        return jnp.concatenate([x1 * cos - x2 * sin, x1 * sin + x2 * cos], axis=1)

    q = rope(q_ref[...].astype(F32))
    k = rope(k_ref[...].astype(F32)) * (RET_QK ** -0.5)
    v = v_ref[...]
    state = state_ref[...]

    intra = lax.dot_general(q.astype(BF16), k.astype(BF16), (((1,), (1,)), ((), ())),
                            preferred_element_type=F32) * decay_ref[...]
    out = (jnp.dot(intra.astype(BF16), v, preferred_element_type=F32)
           + jnp.dot((q * qd_ref[...]).astype(BF16), state.astype(BF16), preferred_element_type=F32))
    kt = (k * kd_ref[...]).T.astype(BF16)
    state_ref[...] = state * cd_ref[...] + jnp.dot(kt, v, preferred_element_type=F32)

    normed = out * _rms(out, RET_V) * on_ref[...]
    o_ref[...] = (_silu(g_ref[...].astype(F32)) * normed).astype(o_ref.dtype)


def retention(h, cos, sin, out_norm, batch, seq, *, c=512):
    n = h.shape[0]
    c = min(c, seq)
    nc = seq // c
    nh = RET_HEADS
    kernel = functools.partial(_ret_kernel, c=c)
    return pl.pallas_call(
        kernel,
        out_shape=jax.ShapeDtypeStruct((n, nh * RET_V), BF16),
        grid=(batch, nh, nc),
        in_specs=[pl.BlockSpec((c, RET_QK), lambda b, hh, i: (b * nc + i, hh)),
                  pl.BlockSpec((c, RET_QK), lambda b, hh, i: (b * nc + i, nh + hh)),
                  pl.BlockSpec((c, RET_V), lambda b, hh, i: (b * nc + i, nh + hh)),
                  pl.BlockSpec((c, RET_V), lambda b, hh, i: (b * nc + i, 2 * nh + hh)),
                  pl.BlockSpec((c, LANES), lambda b, hh, i: (b * nc + i, 0)),
                  pl.BlockSpec((c, LANES), lambda b, hh, i: (b * nc + i, 0)),
                  pl.BlockSpec((1, RET_V), lambda b, hh, i: (0, hh))],
        out_specs=pl.BlockSpec((c, RET_V), lambda b, hh, i: (b * nc + i, hh)),
        scratch_shapes=[pltpu.VMEM((RET_QK, RET_V), F32), pltpu.VMEM((c, c), F32),
                        pltpu.VMEM((c, 1), F32), pltpu.VMEM((c, 1), F32), pltpu.VMEM((1, 1), F32)],
        compiler_params=_cparams(("arbitrary", "arbitrary", "arbitrary")),
        name="retention",
    )(h, h, h, h, cos, sin, out_norm.reshape(1, nh * RET_V))


def _rope_cols(w):
    half = MLA_ROPE // 2
    x1, x2 = w[..., :half], w[..., half:]
    z = jnp.zeros(w.shape[:-1] + (LANES - MLA_ROPE,), w.dtype)
    return jnp.concatenate([x1, x2, z, x2, x1, z], axis=-1)


def _mla_weights(w_in, w_uq, w_ukv):
    w_in, w_uq, w_ukv = w_in.astype(BF16), w_uq.astype(BF16), w_ukv.astype(BF16)
    q_end = MLA_Q_RANK
    kv_end = q_end + MLA_KV_RANK
    r_end = kv_end + MLA_ROPE
    w_in_x = jnp.concatenate([w_in[:, r_end:], w_in[:, :kv_end], _rope_cols(w_in[:, kv_end:r_end])], axis=1)
    uq = w_uq.reshape(MLA_Q_RANK, MLA_HEADS, MLA_NOPE + MLA_ROPE)
    wq = jnp.concatenate([uq[..., :MLA_NOPE], _rope_cols(uq[..., MLA_NOPE:])], axis=-1)
    wq = wq.reshape(MLA_Q_RANK, MLA_HEADS * 3 * LANES)
    ukv = w_ukv.reshape(MLA_KV_RANK, MLA_HEADS, MLA_NOPE + MLA_V)
    wk = ukv[..., :MLA_NOPE].reshape(MLA_KV_RANK, MLA_HEADS * MLA_NOPE)
    wvt = ukv[..., MLA_NOPE:].reshape(MLA_KV_RANK, MLA_HEADS * MLA_V).T
    return w_in_x, wq, wk, wvt


def _rope_gain(g):
    both = _rope_cols(g.reshape(1, MLA_ROPE))
    return both[:, :LANES], both[:, LANES:]


def kernel(x, positions, norm_g, sb_w_in, sb_w_out, sc_w_in, sc_conv_w, sc_conv_b, sc_w_out,
           mla_w_in, mla_q_norm, mla_w_uq, mla_kv_norm, mla_w_ukv, mla_qn_nope, mla_qn_rope,
           mla_kn_nope, mla_kn_rope, mla_w_out, ret_w_in, ret_out_norm, ret_w_out):
    batch, seq, d = x.shape
    n = batch * seq
    depth = norm_g.shape[0]
    xs = x.reshape(n, d)
    pos = positions.reshape(n, 1).astype(F32)
    mla_cos, mla_sin = rope_tables(pos, MLA_ROPE // 2)
    ret_cos, ret_sin = rope_tables(pos, RET_QK // 2)

    for i in range(depth):
        kind, j = i % 4, i // 4
        if kind == 0:
            q_scale = math.log2(math.e) / math.sqrt(SB_DIM)
            col = lax.broadcasted_iota(jnp.int32, (1, sb_w_in.shape[-1]), 1)
            w_in = sb_w_in[j] * jnp.where(col < SB_HEADS * SB_DIM, q_scale, 1.0)
            h = norm_matmul(xs, norm_g[i], w_in.astype(BF16))
            z = sb_attention(h, batch, seq)
            w_out = sb_w_out[j]
        elif kind == 1:
            h = norm_matmul(xs, norm_g[i], sc_w_in[j].astype(BF16))
            z = short_conv(h, sc_conv_w[j], sc_conv_b[j], seq)
            w_out = sc_w_out[j]
        elif kind == 2:
            w_in_x, wq, wk, wvt = _mla_weights(mla_w_in[j], mla_w_uq[j], mla_w_ukv[j])
            h = norm_matmul(xs, norm_g[i], w_in_x)
            qr, qrs = _rope_gain(mla_qn_rope[j])
            kr, krs = _rope_gain(mla_kn_rope[j])
            gains = [mla_q_norm[j].reshape(1, -1), mla_kv_norm[j].reshape(1, -1),
                     mla_qn_nope[j].reshape(1, -1), qr, qrs, mla_kn_nope[j].reshape(1, -1), kr, krs]
            tk = min(MLA_TK, seq // 2)
            qf, kf, vt = mla_prep(h, wq, wk, wvt, gains, mla_cos, mla_sin, tm=tk)
            z = mla_attention(qf, kf, vt, h, batch, seq, tq=2 * tk, tk=tk)
            w_out = mla_w_out[j]
        else:
            h = norm_matmul(xs, norm_g[i], ret_w_in[j].astype(BF16))
            z = retention(h, ret_cos, ret_sin, ret_out_norm[j], batch, seq)
            w_out = ret_w_out[j]
        xs = matmul_residual(z, w_out.astype(BF16), xs)
    return xs.reshape(batch, seq, d)
```

```python
import functools
import math

import jax
import jax.numpy as jnp
from jax import lax
from jax.experimental import pallas as pl
from jax.experimental.pallas import tpu as pltpu

F32 = jnp.float32
BF16 = jnp.bfloat16

D_MODEL = 2048
EPS = 1e-6
ROPE_BASE = 10000.0
CHUNK = 64

LANES = 128
SB_HEADS, SB_DIM = 16, 128
SC_WIDTH = 2048
MLA_HEADS, MLA_NOPE, MLA_ROPE, MLA_V = 16, 128, 64, 128
MLA_Q_RANK, MLA_KV_RANK = 512, 256
MLA_QK_PAD = 256
MLA_TK = 512
RET_HEADS, RET_QK, RET_V = 8, 256, 512

VMEM_LIMIT = 56 * 1024 * 1024

SB_T = 256
SB_GROUPS = 4
EXP2_ZERO_BELOW = -150.0


def _cparams(sem):
    return pltpu.CompilerParams(dimension_semantics=sem, vmem_limit_bytes=VMEM_LIMIT)


def _silu(x):
    return x * (1.0 / (1.0 + jnp.exp(-x)))


def _norm_matmul_kernel(x_ref, g_ref, w_ref, o_ref, xn_ref, *, sub):
    first_col = pl.program_id(1) == 0

    @pl.when(first_col)
    def _():
        for r in range(0, x_ref.shape[0], sub):
            x = x_ref[r:r + sub, :]
            ms = jnp.mean(x * x, axis=-1, keepdims=True)
            xn = (x * lax.rsqrt(ms + EPS) * g_ref[...]).astype(BF16)
            xn_ref[r:r + sub, :] = xn
            o_ref[r:r + sub, :] = jnp.dot(xn, w_ref[...], preferred_element_type=F32).astype(o_ref.dtype)

    @pl.when(jnp.logical_not(first_col))
    def _():
        o_ref[...] = jnp.dot(xn_ref[...], w_ref[...], preferred_element_type=F32).astype(o_ref.dtype)


def norm_matmul(x, g, w, *, tm=1024):
    n, d = x.shape
    dout = w.shape[1]
    tm = min(tm, n)
    tn = next(c for c in (2048, 1536, 1024, dout) if dout % c == 0)
    return pl.pallas_call(
        functools.partial(_norm_matmul_kernel, sub=min(256, tm)),
        out_shape=jax.ShapeDtypeStruct((n, dout), BF16),
        grid=(n // tm, dout // tn),
        in_specs=[pl.BlockSpec((tm, d), lambda i, j: (i, 0)),
                  pl.BlockSpec((1, d), lambda i, j: (0, 0)),
                  pl.BlockSpec((d, tn), lambda i, j: (0, j))],
        out_specs=pl.BlockSpec((tm, tn), lambda i, j: (i, j)),
        scratch_shapes=[pltpu.VMEM((tm, d), BF16)],
        compiler_params=_cparams(("arbitrary", "arbitrary")),
        name="norm_matmul",
    )(x, g.reshape(1, d), w)


def _matmul_residual_kernel(z_ref, w_ref, x_ref, o_ref):
    o_ref[...] = x_ref[...] + jnp.dot(z_ref[...], w_ref[...], preferred_element_type=F32)


def matmul_residual(z, w, x, *, tm=512):
    n, k = z.shape
    dout = w.shape[1]
    tm = min(tm, n)
    return pl.pallas_call(
        _matmul_residual_kernel,
        out_shape=jax.ShapeDtypeStruct((n, dout), F32),
        grid=(n // tm,),
        in_specs=[pl.BlockSpec((tm, k), lambda i: (i, 0)),
                  pl.BlockSpec((k, dout), lambda i: (0, 0), pipeline_mode=pl.Buffered(1)),
                  pl.BlockSpec((tm, dout), lambda i: (i, 0))],
        out_specs=pl.BlockSpec((tm, dout), lambda i: (i, 0)),
        compiler_params=_cparams(("arbitrary",)),
        name="matmul_residual",
    )(z, w, x)


def _rope_table_kernel(pos_ref, cos_ref, sin_ref, *, half):
    lane = lax.broadcasted_iota(jnp.int32, (1, LANES), 1) % half
    inv_freq = jnp.exp(lane.astype(F32) * (-math.log(ROPE_BASE) / half))
    ang = pos_ref[...] * inv_freq
    cos_ref[...] = jnp.cos(ang)
    sin_ref[...] = jnp.sin(ang)


def rope_tables(pos, half, *, tm=1024):
    n = pos.shape[0]
    tm = min(tm, n)
    spec = pl.BlockSpec((tm, LANES), lambda i: (i, 0))
    return pl.pallas_call(
        functools.partial(_rope_table_kernel, half=half),
        out_shape=[jax.ShapeDtypeStruct((n, LANES), F32)] * 2,
        grid=(n // tm,),
        in_specs=[pl.BlockSpec((tm, 1), lambda i: (i, 0))],
        out_specs=[spec, spec],
        compiler_params=_cparams(("arbitrary",)),
        name="rope_tables",
    )(pos)


def _sb_kernel(q_ref, k_ref, v_ref, g_ref, o_ref, vt_ref, acc_ref, r_ref, *, t, groups):
    qi = pl.program_id(2)

    @pl.when(qi == 0)
    def _():
        for j in range(vt_ref.shape[0]):
            vt_ref[j] = v_ref[j * t:(j + 1) * t, :].astype(F32).T.astype(BF16)

    acc_ref[...] = jnp.zeros_like(acc_ref)
    tq = groups * t
    key = lax.broadcasted_iota(jnp.int32, (t, t), 0)
    from_here = (lax.broadcasted_iota(jnp.int32, (t, t), 1) >= key).astype(BF16)
    lane = lax.broadcasted_iota(jnp.int32, (1, tq), 1)
    first = groups * qi

    group_lanes = [slice(g * t, (g + 1) * t) for g in range(groups)]
    strict = key < lax.broadcasted_iota(jnp.int32, (t, t), 1)

    def block_ids(d):
        return [jnp.maximum(first + g - d, 0) for g in range(groups)]

    def scan_keys(d, diag):
        zs = [lax.dot_general(k_ref[pl.ds(pl.multiple_of(kb * t, t), t), :], q_ref[lanes, :],
                              (((1,), (1,)), ((), ())), preferred_element_type=F32)
              for kb, lanes in zip(block_ids(d), group_lanes)]
        tails = []
        for z in zs:
            sp = jnp.maximum(z, 0.0) + jnp.log2(1.0 + jnp.exp2(-jnp.abs(z)))
            if diag:
                sp = jnp.where(strict, sp, 0.0)
            tails.append(jnp.dot(from_here, sp.astype(BF16), preferred_element_type=F32))
        return zs, tails

    def add_values(d, diag, r, scanned):
        zs, tails = scanned
        r = jnp.where(lane // t >= d - first, r, -jnp.inf)
        r_new = []
        for g, (kb, lanes) in enumerate(zip(block_ids(d), group_lanes)):
            a = jnp.exp2(zs[g] + r[:, lanes] - tails[g])
            if diag:
                a = jnp.where(strict, a, 0.0)
            acc_ref[:, lanes] += jnp.dot(vt_ref[kb], a.astype(BF16), preferred_element_type=F32)
            r_new.append(r[:, lanes] - tails[g][0:1, :])
        return jnp.concatenate(r_new, axis=1)

    scanned0 = scan_keys(0, True)
    scanned1 = scan_keys(1, False)
    r1 = add_values(0, True, jnp.zeros((1, tq), F32), scanned0)
    r2 = add_values(1, False, r1, scanned1)
    r_ref[...] = r2

    def cond(carry):
        d, rmax = carry
        return jnp.logical_and(d <= first + groups - 1, rmax >= EXP2_ZERO_BELOW)

    def body(carry):
        d, _ = carry
        r_new = add_values(d, False, r_ref[...], scan_keys(d, False))
        r_ref[...] = r_new
        return d + 1, jnp.max(r_new)

    lax.while_loop(cond, body, (jnp.int32(2), jnp.max(r2)))
    o_ref[...] = (acc_ref[...].T * _silu(g_ref[...].astype(F32))).astype(o_ref.dtype)


def sb_attention(h, batch, seq):
    n = h.shape[0]
    t = min(SB_T, seq)
    groups = min(SB_GROUPS, seq // t)
    tq = t * groups
    nq = seq // tq
    nh = SB_HEADS
    kernel = functools.partial(_sb_kernel, t=t, groups=groups)
    return pl.pallas_call(
        kernel,
        out_shape=jax.ShapeDtypeStruct((n, nh * SB_DIM), BF16),
        grid=(batch, nh, nq),
        in_specs=[pl.BlockSpec((tq, SB_DIM), lambda b, hh, i: (b * nq + i, hh)),
                  pl.BlockSpec((seq, SB_DIM), lambda b, hh, i: (b, nh + hh)),
                  pl.BlockSpec((seq, SB_DIM), lambda b, hh, i: (b, 2 * nh + hh)),
                  pl.BlockSpec((tq, SB_DIM), lambda b, hh, i: (b * nq + i, 3 * nh + hh))],
        out_specs=pl.BlockSpec((tq, SB_DIM), lambda b, hh, i: (b * nq + i, hh)),
        scratch_shapes=[pltpu.VMEM((seq // t, SB_DIM, t), BF16), pltpu.VMEM((SB_DIM, tq), F32),
                        pltpu.VMEM((1, tq), F32)],
        compiler_params=_cparams(("arbitrary", "arbitrary", "arbitrary")),
        name="sb_attention",
    )(h, h, h, h)


def _conv_kernel(b_ref, c_ref, u_ref, g_ref, w_ref, bias_ref, o_ref, halo_ref, *, tm, blocks_per_seq):
    i = pl.program_id(0)
    @pl.when(i % blocks_per_seq == 0)
    def _():
        halo_ref[...] = jnp.zeros_like(halo_ref)

    cu = c_ref[...].astype(F32) * u_ref[...].astype(F32)
    prev = halo_ref[...]
    row = lax.broadcasted_iota(jnp.int32, (tm, 1), 0)
    back1 = jnp.where(row == 0, prev[7:8, :], pltpu.roll(cu, 1, axis=0))
    back2 = jnp.where(row == 0, prev[6:7, :],
                      jnp.where(row == 1, prev[7:8, :], pltpu.roll(cu, 2, axis=0)))
    w = w_ref[...]
    y = w[0:1, :] * back2 + w[1:2, :] * back1 + w[2:3, :] * cu + bias_ref[...]
    halo_ref[...] = cu[tm - 8:, :]
    o_ref[...] = (b_ref[...].astype(F32) * y * _silu(g_ref[...].astype(F32))).astype(o_ref.dtype)


def short_conv(h, conv_w, conv_b, seq, *, tm=512):
    n = h.shape[0]
    w = SC_WIDTH
    tm = min(tm, seq)
    kernel = functools.partial(_conv_kernel, tm=tm, blocks_per_seq=seq // tm)
    col = lambda j: pl.BlockSpec((tm, w), lambda i: (i, j))
    return pl.pallas_call(
        kernel,
        out_shape=jax.ShapeDtypeStruct((n, w), BF16),
        grid=(n // tm,),
        in_specs=[col(0), col(1), col(2), col(3),
                  pl.BlockSpec((3, w), lambda i: (0, 0)),
                  pl.BlockSpec((1, w), lambda i: (0, 0))],
        out_specs=pl.BlockSpec((tm, w), lambda i: (i, 0)),
        scratch_shapes=[pltpu.VMEM((8, w), F32)],
        compiler_params=_cparams(("arbitrary",)),
        name="short_conv",
    )(h, h, h, h, conv_w, conv_b.reshape(1, w))


def _rms(x, width):
    return lax.rsqrt(jnp.sum(x * x, axis=-1, keepdims=True) * (1.0 / width) + EPS)


def _mla_prep_kernel(c_ref, wq_ref, wk_ref, wvt_ref, qg_ref, kvg_ref, qn_ref, qr_ref, qrs_ref,
                     kn_ref, kr_ref, krs_ref, cos_ref, sin_ref,
                     q_out, k_out, vt_out, *, scale):
    c = c_ref[...].astype(F32)
    c_q = c[:, :MLA_Q_RANK]
    c_kv = c[:, MLA_Q_RANK:MLA_Q_RANK + MLA_KV_RANK]
    kro = c[:, MLA_Q_RANK + MLA_KV_RANK:MLA_Q_RANK + MLA_KV_RANK + LANES]
    kro_s = c[:, MLA_Q_RANK + MLA_KV_RANK + LANES:]
    cqn = (c_q * _rms(c_q, MLA_Q_RANK) * qg_ref[...]).astype(BF16)
    ckn = (c_kv * _rms(c_kv, MLA_KV_RANK) * kvg_ref[...]).astype(BF16)

    cos = cos_ref[...]
    sin = sin_ref[...]
    lane = lax.broadcasted_iota(jnp.int32, (1, LANES), 1)
    sin_signed = jnp.where(lane < MLA_ROPE // 2, -sin, sin)

    k_rot = (kro * kr_ref[...] * cos + kro_s * krs_ref[...] * sin_signed) * _rms(kro, MLA_ROPE)
    k_rot = k_rot.astype(k_out.dtype)

    vt = lax.dot_general(wvt_ref[...], ckn, (((1,), (1,)), ((), ())), preferred_element_type=F32)
    vt_out[0] = vt.reshape(vt_out.shape[1:]).astype(vt_out.dtype)

    for pair in range(MLA_HEADS // 2):
        q6 = jnp.dot(cqn, wq_ref[:, pair * 6 * LANES:(pair + 1) * 6 * LANES], preferred_element_type=F32)
        kn2 = jnp.dot(ckn, wk_ref[:, pair * 2 * LANES:(pair + 1) * 2 * LANES], preferred_element_type=F32)
        for sub in range(2):
            hh = 2 * pair + sub
            qn = q6[:, sub * 3 * LANES:sub * 3 * LANES + LANES]
            qr = q6[:, sub * 3 * LANES + LANES:sub * 3 * LANES + 2 * LANES]
            qs = q6[:, sub * 3 * LANES + 2 * LANES:(sub + 1) * 3 * LANES]
            qn = qn * _rms(qn, MLA_NOPE) * qn_ref[...]
            q_rot = (qr * qr_ref[...] * cos + qs * qrs_ref[...] * sin_signed) * _rms(qr, MLA_ROPE)
            q_out[:, hh * MLA_QK_PAD:hh * MLA_QK_PAD + LANES] = (qn * scale).astype(q_out.dtype)
            q_out[:, hh * MLA_QK_PAD + LANES:(hh + 1) * MLA_QK_PAD] = (q_rot * scale).astype(q_out.dtype)

            kn = kn2[:, sub * LANES:(sub + 1) * LANES]
            kn = kn * _rms(kn, MLA_NOPE) * kn_ref[...]
            k_out[:, hh * MLA_QK_PAD:hh * MLA_QK_PAD + LANES] = kn.astype(k_out.dtype)
            k_out[:, hh * MLA_QK_PAD + LANES:(hh + 1) * MLA_QK_PAD] = k_rot


def mla_prep(h, wq, wk, wvt, gains, cos, sin, *, tm):
    n = h.shape[0]
    cw = 1024
    row = lambda width, j=0: pl.BlockSpec((tm, width), lambda i: (i, j))
    full = lambda a: pl.BlockSpec(a.shape, lambda i: (0, 0))
    kernel = functools.partial(_mla_prep_kernel, scale=math.log2(math.e) / math.sqrt(MLA_NOPE + MLA_ROPE))
    return pl.pallas_call(
        kernel,
        out_shape=[jax.ShapeDtypeStruct((n, MLA_HEADS * MLA_QK_PAD), BF16),
                   jax.ShapeDtypeStruct((n, MLA_HEADS * MLA_QK_PAD), BF16),
                   jax.ShapeDtypeStruct((n // tm, MLA_HEADS, MLA_V, tm), BF16)],
        grid=(n // tm,),
        in_specs=[row(cw, 2), full(wq), full(wk), full(wvt)] + [full(a) for a in gains]
                 + [row(LANES), row(LANES)],
        out_specs=[row(MLA_HEADS * MLA_QK_PAD), row(MLA_HEADS * MLA_QK_PAD),
                   pl.BlockSpec((1, MLA_HEADS, MLA_V, tm), lambda i: (i, 0, 0, 0))],
        compiler_params=_cparams(("arbitrary",)),
        name="mla_prep",
    )(h, wq, wk, wvt, *gains, cos, sin)


def _mla_attn_kernel(q_ref, k_ref, vt_ref, g_ref, o_ref, s0_ref, s1_ref, m_ref, l_ref, acc_ref, *, tq, tk):
    qi = pl.program_id(2)
    m_ref[...] = jnp.full_like(m_ref, -jnp.inf)
    l_ref[...] = jnp.zeros_like(l_ref)
    acc_ref[...] = jnp.zeros_like(acc_ref)

    def scores(kb, dst, lo=0):
        ks = pl.multiple_of(kb * tk, tk)
        dst[:, lo:] = lax.dot_general(k_ref[pl.ds(ks, tk), :], q_ref[lo:, :], (((1,), (1,)), ((), ())),
                                      preferred_element_type=F32)

    def consume(src, kb, lo=0, diag=None):
        s = src[:, lo:]
        if diag is not None:
            key = diag + lax.broadcasted_iota(jnp.int32, s.shape, 0)
            qry = lo + lax.broadcasted_iota(jnp.int32, s.shape, 1)
            s = jnp.where(key // CHUNK <= qry // CHUNK, s, -jnp.inf)
        m_old = m_ref[:, lo:]
        m_new = jnp.maximum(m_old, jnp.max(s, axis=0, keepdims=True))
        alpha = jnp.exp2(m_old - m_new)
        p = jnp.exp2(s - m_new)
        l_ref[:, lo:] = alpha * l_ref[:, lo:] + jnp.sum(p, axis=0, keepdims=True)
        acc_ref[:, lo:] = alpha * acc_ref[:, lo:] + jnp.dot(vt_ref[kb, 0], p.astype(BF16),
                                                            preferred_element_type=F32)
        m_ref[:, lo:] = m_new

    def body(j, carry):
        scores(2 * j + 1, s1_ref)
        consume(s0_ref, 2 * j)
        scores(2 * j + 2, s0_ref)
        consume(s1_ref, 2 * j + 1)
        return carry

    scores(0, s0_ref)
    lax.fori_loop(0, qi, body, 0)
    scores(2 * qi + 1, s1_ref, lo=tk)
    consume(s0_ref, 2 * qi, diag=0)
    consume(s1_ref, 2 * qi + 1, lo=tk, diag=tk)
    o = (acc_ref[...] * (1.0 / l_ref[...])).T
    o_ref[...] = (o * _silu(g_ref[...].astype(F32))).astype(o_ref.dtype)


def mla_attention(qf, kf, vt, h, batch, seq, *, tq, tk):
    n = qf.shape[0]
    nq = seq // tq
    nkb = seq // tk
    kernel = functools.partial(_mla_attn_kernel, tq=tq, tk=tk)
    return pl.pallas_call(
        kernel,
        out_shape=jax.ShapeDtypeStruct((n, MLA_HEADS * MLA_V), BF16),
        grid=(batch, MLA_HEADS, nq),
        in_specs=[pl.BlockSpec((tq, MLA_QK_PAD), lambda b, hh, i: (b * nq + i, hh)),
                  pl.BlockSpec((seq, MLA_QK_PAD), lambda b, hh, i: (b, hh)),
                  pl.BlockSpec((nkb, 1, MLA_V, tk), lambda b, hh, i: (b, hh, 0, 0)),
                  pl.BlockSpec((tq, MLA_V), lambda b, hh, i: (b * nq + i, hh))],
        out_specs=pl.BlockSpec((tq, MLA_V), lambda b, hh, i: (b * nq + i, hh)),
        scratch_shapes=[pltpu.VMEM((tk, tq), F32), pltpu.VMEM((tk, tq), F32),
                        pltpu.VMEM((1, tq), F32), pltpu.VMEM((1, tq), F32), pltpu.VMEM((MLA_V, tq), F32)],
        compiler_params=_cparams(("arbitrary", "arbitrary", "arbitrary")),
        name="mla_attention",
    )(qf, kf, vt, h)


def _ret_kernel(q_ref, k_ref, v_ref, g_ref, cos_ref, sin_ref, on_ref, o_ref,
                state_ref, decay_ref, qd_ref, kd_ref, cd_ref, *, c):
    hh = pl.program_id(1)
    ci = pl.program_id(2)

    @pl.when(ci == 0)
    def _():
        state_ref[...] = jnp.zeros_like(state_ref)
        head = jnp.full((1, 1), hh, jnp.int32).astype(F32)
        log_gamma = jnp.log1p(-jnp.exp2(-5.0 - head))
        n_row = lax.broadcasted_iota(jnp.int32, (c, c), 0)
        n_col = lax.broadcasted_iota(jnp.int32, (c, c), 1)
        rel = (n_row - n_col).astype(F32)
        decay_ref[...] = jnp.where(rel >= 0, jnp.exp(log_gamma * jnp.maximum(rel, 0.0)), 0.0)
        n1 = lax.broadcasted_iota(jnp.int32, (c, 1), 0).astype(F32)
        qd_ref[...] = jnp.exp(log_gamma * (n1 + 1.0))
        kd_ref[...] = jnp.exp(log_gamma * (c - 1.0 - n1))
        cd_ref[...] = jnp.exp(log_gamma * c)

    cos = cos_ref[...]
    sin = sin_ref[...]
    half = RET_QK // 2

    def rope(x):
        x1, x2 = x[:, :half], x[:, half:]
        return jnp.concatenate([x1 * cos - x2 * sin, x1 * sin + x2 * cos], axis=1)

    q = rope(q_ref[...].astype(F32))
    k = rope(k_ref[...].astype(F32)) * (RET_QK ** -0.5)
    v = v_ref[...]
    state = state_ref[...]

    intra = lax.dot_general(q.astype(BF16), k.astype(BF16), (((1,), (1,)), ((), ())),
                            preferred_element_type=F32) * decay_ref[...]
    out = (jnp.dot(intra.astype(BF16), v, preferred_element_type=F32)
           + jnp.dot((q * qd_ref[...]).astype(BF16), state.astype(BF16), preferred_element_type=F32))
    kt = (k * kd_ref[...]).T.astype(BF16)
    state_ref[...] = state * cd_ref[...] + jnp.dot(kt, v, preferred_element_type=F32)

    normed = out * _rms(out, RET_V) * on_ref[...]
    o_ref[...] = (_silu(g_ref[...].astype(F32)) * normed).astype(o_ref.dtype)


def retention(h, cos, sin, out_norm, batch, seq, *, c=512):
    n = h.shape[0]
    c = min(c, seq)
    nc = seq // c
    nh = RET_HEADS
    kernel = functools.partial(_ret_kernel, c=c)
    return pl.pallas_call(
        kernel,
        out_shape=jax.ShapeDtypeStruct((n, nh * RET_V), BF16),
        grid=(batch, nh, nc),
        in_specs=[pl.BlockSpec((c, RET_QK), lambda b, hh, i: (b * nc + i, hh)),
                  pl.BlockSpec((c, RET_QK), lambda b, hh, i: (b * nc + i, nh + hh)),
                  pl.BlockSpec((c, RET_V), lambda b, hh, i: (b * nc + i, nh + hh)),
                  pl.BlockSpec((c, RET_V), lambda b, hh, i: (b * nc + i, 2 * nh + hh)),
                  pl.BlockSpec((c, LANES), lambda b, hh, i: (b * nc + i, 0)),
                  pl.BlockSpec((c, LANES), lambda b, hh, i: (b * nc + i, 0)),
                  pl.BlockSpec((1, RET_V), lambda b, hh, i: (0, hh))],
        out_specs=pl.BlockSpec((c, RET_V), lambda b, hh, i: (b * nc + i, hh)),
        scratch_shapes=[pltpu.VMEM((RET_QK, RET_V), F32), pltpu.VMEM((c, c), F32),
                        pltpu.VMEM((c, 1), F32), pltpu.VMEM((c, 1), F32), pltpu.VMEM((1, 1), F32)],
        compiler_params=_cparams(("arbitrary", "arbitrary", "arbitrary")),
        name="retention",
    )(h, h, h, h, cos, sin, out_norm.reshape(1, nh * RET_V))


def _rope_cols(w):
    half = MLA_ROPE // 2
    x1, x2 = w[..., :half], w[..., half:]
    z = jnp.zeros(w.shape[:-1] + (LANES - MLA_ROPE,), w.dtype)
    return jnp.concatenate([x1, x2, z, x2, x1, z], axis=-1)


def _mla_weights(w_in, w_uq, w_ukv):
    w_in, w_uq, w_ukv = w_in.astype(BF16), w_uq.astype(BF16), w_ukv.astype(BF16)
    q_end = MLA_Q_RANK
    kv_end = q_end + MLA_KV_RANK
    r_end = kv_end + MLA_ROPE
    w_in_x = jnp.concatenate([w_in[:, r_end:], w_in[:, :kv_end], _rope_cols(w_in[:, kv_end:r_end])], axis=1)
    uq = w_uq.reshape(MLA_Q_RANK, MLA_HEADS, MLA_NOPE + MLA_ROPE)
    wq = jnp.concatenate([uq[..., :MLA_NOPE], _rope_cols(uq[..., MLA_NOPE:])], axis=-1)
    wq = wq.reshape(MLA_Q_RANK, MLA_HEADS * 3 * LANES)
    ukv = w_ukv.reshape(MLA_KV_RANK, MLA_HEADS, MLA_NOPE + MLA_V)
    wk = ukv[..., :MLA_NOPE].reshape(MLA_KV_RANK, MLA_HEADS * MLA_NOPE)
    wvt = ukv[..., MLA_NOPE:].reshape(MLA_KV_RANK, MLA_HEADS * MLA_V).T
    return w_in_x, wq, wk, wvt


def _rope_gain(g):
    both = _rope_cols(g.reshape(1, MLA_ROPE))
    return both[:, :LANES], both[:, LANES:]


def kernel(x, positions, norm_g, sb_w_in, sb_w_out, sc_w_in, sc_conv_w, sc_conv_b, sc_w_out,
           mla_w_in, mla_q_norm, mla_w_uq, mla_kv_norm, mla_w_ukv, mla_qn_nope, mla_qn_rope,
           mla_kn_nope, mla_kn_rope, mla_w_out, ret_w_in, ret_out_norm, ret_w_out):
    batch, seq, d = x.shape
    n = batch * seq
    depth = norm_g.shape[0]
    xs = x.reshape(n, d)
    pos = positions.reshape(n, 1).astype(F32)
    mla_cos, mla_sin = rope_tables(pos, MLA_ROPE // 2)
    ret_cos, ret_sin = rope_tables(pos, RET_QK // 2)

    for i in range(depth):
        kind, j = i % 4, i // 4
        if kind == 0:
            q_scale = math.log2(math.e) / math.sqrt(SB_DIM)
            col = lax.broadcasted_iota(jnp.int32, (1, sb_w_in.shape[-1]), 1)
            w_in = sb_w_in[j] * jnp.where(col < SB_HEADS * SB_DIM, q_scale, 1.0)
            h = norm_matmul(xs, norm_g[i], w_in.astype(BF16))
            z = sb_attention(h, batch, seq)
            w_out = sb_w_out[j]
        elif kind == 1:
            h = norm_matmul(xs, norm_g[i], sc_w_in[j].astype(BF16))
            z = short_conv(h, sc_conv_w[j], sc_conv_b[j], seq)
            w_out = sc_w_out[j]
        elif kind == 2:
            w_in_x, wq, wk, wvt = _mla_weights(mla_w_in[j], mla_w_uq[j], mla_w_ukv[j])
            h = norm_matmul(xs, norm_g[i], w_in_x)
            qr, qrs = _rope_gain(mla_qn_rope[j])
            kr, krs = _rope_gain(mla_kn_rope[j])
            gains = [mla_q_norm[j].reshape(1, -1), mla_kv_norm[j].reshape(1, -1),
                     mla_qn_nope[j].reshape(1, -1), qr, qrs, mla_kn_nope[j].reshape(1, -1), kr, krs]
            tk = min(MLA_TK, seq // 2)
            qf, kf, vt = mla_prep(h, wq, wk, wvt, gains, mla_cos, mla_sin, tm=tk)
            z = mla_attention(qf, kf, vt, h, batch, seq, tq=2 * tk, tk=tk)
            w_out = mla_w_out[j]
        else:
            h = norm_matmul(xs, norm_g[i], ret_w_in[j].astype(BF16))
            z = retention(h, ret_cos, ret_sin, ret_out_norm[j], batch, seq)
            w_out = ret_w_out[j]
        xs = matmul_residual(z, w_out.astype(BF16), xs)
    return xs.reshape(batch, seq, d)
```

```python
import functools
import math

import jax
import jax.numpy as jnp
from jax import lax
from jax.experimental import pallas as pl
from jax.experimental.pallas import tpu as pltpu

F32 = jnp.float32
BF16 = jnp.bfloat16

D_MODEL = 2048
EPS = 1e-6
ROPE_BASE = 10000.0
CHUNK = 64

LANES = 128
SB_HEADS, SB_DIM = 16, 128
SC_WIDTH = 2048
MLA_HEADS, MLA_NOPE, MLA_ROPE, MLA_V = 16, 128, 64, 128
MLA_Q_RANK, MLA_KV_RANK = 512, 256
MLA_QK_PAD = 256
MLA_TK = 512
RET_HEADS, RET_QK, RET_V = 8, 256, 512

VMEM_LIMIT = 56 * 1024 * 1024

SB_T = 256
SB_GROUPS = 4
EXP2_ZERO_BELOW = -150.0


def _cparams(sem):
    return pltpu.CompilerParams(dimension_semantics=sem, vmem_limit_bytes=VMEM_LIMIT)


def _silu(x):
    return x * (1.0 / (1.0 + jnp.exp(-x)))


def _norm_matmul_kernel(x_ref, g_ref, w_ref, o_ref, xn_ref, *, sub):
    first_col = pl.program_id(1) == 0

    @pl.when(first_col)
    def _():
        for r in range(0, x_ref.shape[0], sub):
            x = x_ref[r:r + sub, :]
            ms = jnp.mean(x * x, axis=-1, keepdims=True)
            xn = (x * lax.rsqrt(ms + EPS) * g_ref[...]).astype(BF16)
            xn_ref[r:r + sub, :] = xn
            o_ref[r:r + sub, :] = jnp.dot(xn, w_ref[...], preferred_element_type=F32).astype(o_ref.dtype)

    @pl.when(jnp.logical_not(first_col))
    def _():
        o_ref[...] = jnp.dot(xn_ref[...], w_ref[...], preferred_element_type=F32).astype(o_ref.dtype)


def norm_matmul(x, g, w, *, tm=1024):
    n, d = x.shape
    dout = w.shape[1]
    tm = min(tm, n)
    tn = next(c for c in (2048, 1536, 1024, dout) if dout % c == 0)
    return pl.pallas_call(
        functools.partial(_norm_matmul_kernel, sub=min(256, tm)),
        out_shape=jax.ShapeDtypeStruct((n, dout), BF16),
        grid=(n // tm, dout // tn),
        in_specs=[pl.BlockSpec((tm, d), lambda i, j: (i, 0)),
                  pl.BlockSpec((1, d), lambda i, j: (0, 0)),
                  pl.BlockSpec((d, tn), lambda i, j: (0, j))],
        out_specs=pl.BlockSpec((tm, tn), lambda i, j: (i, j)),
        scratch_shapes=[pltpu.VMEM((tm, d), BF16)],
        compiler_params=_cparams(("arbitrary", "arbitrary")),
        name="norm_matmul",
    )(x, g.reshape(1, d), w)


def _matmul_residual_kernel(z_ref, w_ref, x_ref, o_ref):
    o_ref[...] = x_ref[...] + jnp.dot(z_ref[...], w_ref[...], preferred_element_type=F32)


def matmul_residual(z, w, x, *, tm=512):
    n, k = z.shape
    dout = w.shape[1]
    tm = min(tm, n)
    return pl.pallas_call(
        _matmul_residual_kernel,
        out_shape=jax.ShapeDtypeStruct((n, dout), F32),
        grid=(n // tm,),
        in_specs=[pl.BlockSpec((tm, k), lambda i: (i, 0)),
                  pl.BlockSpec((k, dout), lambda i: (0, 0), pipeline_mode=pl.Buffered(1)),
                  pl.BlockSpec((tm, dout), lambda i: (i, 0))],
        out_specs=pl.BlockSpec((tm, dout), lambda i: (i, 0)),
        compiler_params=_cparams(("arbitrary",)),
        name="matmul_residual",
    )(z, w, x)


def _rope_table_kernel(pos_ref, cos_ref, sin_ref, *, half):
    lane = lax.broadcasted_iota(jnp.int32, (1, LANES), 1) % half
    inv_freq = jnp.exp(lane.astype(F32) * (-math.log(ROPE_BASE) / half))
    ang = pos_ref[...] * inv_freq
    cos_ref[...] = jnp.cos(ang)
    sin_ref[...] = jnp.sin(ang)


def rope_tables(pos, half, *, tm=1024):
    n = pos.shape[0]
    tm = min(tm, n)
    spec = pl.BlockSpec((tm, LANES), lambda i: (i, 0))
    return pl.pallas_call(
        functools.partial(_rope_table_kernel, half=half),
        out_shape=[jax.ShapeDtypeStruct((n, LANES), F32)] * 2,
        grid=(n // tm,),
        in_specs=[pl.BlockSpec((tm, 1), lambda i: (i, 0))],
        out_specs=[spec, spec],
        compiler_params=_cparams(("arbitrary",)),
        name="rope_tables",
    )(pos)


def _sb_kernel(q_ref, k_ref, v_ref, g_ref, o_ref, vt_ref, acc_ref, r_ref, *, t, groups):
    qi = pl.program_id(2)

    @pl.when(qi == 0)
    def _():
        for j in range(vt_ref.shape[0]):
            vt_ref[j] = v_ref[j * t:(j + 1) * t, :].astype(F32).T.astype(BF16)

    acc_ref[...] = jnp.zeros_like(acc_ref)
    tq = groups * t
    key = lax.broadcasted_iota(jnp.int32, (t, t), 0)
    from_here = (lax.broadcasted_iota(jnp.int32, (t, t), 1) >= key).astype(BF16)
    lane = lax.broadcasted_iota(jnp.int32, (1, tq), 1)
    first = groups * qi

    group_lanes = [slice(g * t, (g + 1) * t) for g in range(groups)]
    strict = key < lax.broadcasted_iota(jnp.int32, (t, t), 1)

    def block_ids(d):
        return [jnp.maximum(first + g - d, 0) for g in range(groups)]

    def scan_keys(d, diag):
        zs = [lax.dot_general(k_ref[pl.ds(pl.multiple_of(kb * t, t), t), :], q_ref[lanes, :],
                              (((1,), (1,)), ((), ())), preferred_element_type=F32)
              for kb, lanes in zip(block_ids(d), group_lanes)]
        tails = []
        for z in zs:
            sp = jnp.maximum(z, 0.0) + jnp.log2(1.0 + jnp.exp2(-jnp.abs(z)))
            if diag:
                sp = jnp.where(strict, sp, 0.0)
            tails.append(jnp.dot(from_here, sp.astype(BF16), preferred_element_type=F32))
        return zs, tails

    def add_values(d, diag, r, scanned):
        zs, tails = scanned
        r = jnp.where(lane // t >= d - first, r, -jnp.inf)
        r_new = []
        for g, (kb, lanes) in enumerate(zip(block_ids(d), group_lanes)):
            a = jnp.exp2(zs[g] + r[:, lanes] - tails[g])
            if diag:
                a = jnp.where(strict, a, 0.0)
            acc_ref[:, lanes] += jnp.dot(vt_ref[kb], a.astype(BF16), preferred_element_type=F32)
            r_new.append(r[:, lanes] - tails[g][0:1, :])
        return jnp.concatenate(r_new, axis=1)

    scanned0 = scan_keys(0, True)
    scanned1 = scan_keys(1, False)
    r1 = add_values(0, True, jnp.zeros((1, tq), F32), scanned0)
    r2 = add_values(1, False, r1, scanned1)
    r_ref[...] = r2

    def cond(carry):
        d, rmax = carry
        return jnp.logical_and(d <= first + groups - 1, rmax >= EXP2_ZERO_BELOW)

    def body(carry):
        d, _ = carry
        r_new = add_values(d, False, r_ref[...], scan_keys(d, False))
        r_ref[...] = r_new
        return d + 1, jnp.max(r_new)

    lax.while_loop(cond, body, (jnp.int32(2), jnp.max(r2)))
    o_ref[...] = (acc_ref[...].T * _silu(g_ref[...].astype(F32))).astype(o_ref.dtype)


def sb_attention(h, batch, seq):
    n = h.shape[0]
    t = min(SB_T, seq)
    groups = min(SB_GROUPS, seq // t)
    tq = t * groups
    nq = seq // tq
    nh = SB_HEADS
    kernel = functools.partial(_sb_kernel, t=t, groups=groups)
    return pl.pallas_call(
        kernel,
        out_shape=jax.ShapeDtypeStruct((n, nh * SB_DIM), BF16),
        grid=(batch, nh, nq),
        in_specs=[pl.BlockSpec((tq, SB_DIM), lambda b, hh, i: (b * nq + i, hh)),
                  pl.BlockSpec((seq, SB_DIM), lambda b, hh, i: (b, nh + hh)),
                  pl.BlockSpec((seq, SB_DIM), lambda b, hh, i: (b, 2 * nh + hh)),
                  pl.BlockSpec((tq, SB_DIM), lambda b, hh, i: (b * nq + i, 3 * nh + hh))],
        out_specs=pl.BlockSpec((tq, SB_DIM), lambda b, hh, i: (b * nq + i, hh)),
        scratch_shapes=[pltpu.VMEM((seq // t, SB_DIM, t), BF16), pltpu.VMEM((SB_DIM, tq), F32),
                        pltpu.VMEM((1, tq), F32)],
        compiler_params=_cparams(("arbitrary", "arbitrary", "arbitrary")),
        name="sb_attention",
    )(h, h, h, h)


SC_TILE = 256


def _norm_conv_kernel(x_ref, g_ref, w_ref, cw_ref, cb_ref, o_ref, xn_ref, halo_ref, *,
                      sub, blocks_per_seq):
    i = pl.program_id(0)
    j = pl.program_id(1)

    @pl.when(i % blocks_per_seq == 0)
    def _():
        halo_ref[j] = jnp.zeros(halo_ref.shape[1:], F32)

    def run(normalise):
        cw = cw_ref[...]
        bias = cb_ref[...]
        row = lax.broadcasted_iota(jnp.int32, (sub, 1), 0)
        prev = halo_ref[j]
        for r in range(0, x_ref.shape[0], sub):
            if normalise:
                x = x_ref[r:r + sub, :]
                ms = jnp.mean(x * x, axis=-1, keepdims=True)
                xn = (x * lax.rsqrt(ms + EPS) * g_ref[...]).astype(BF16)
                xn_ref[r:r + sub, :] = xn
            else:
                xn = xn_ref[r:r + sub, :]
            res = jnp.dot(xn, w_ref[...], preferred_element_type=F32)
            b_gate, c_gate = res[:, :SC_TILE], res[:, SC_TILE:2 * SC_TILE]
            u, gate = res[:, 2 * SC_TILE:3 * SC_TILE], res[:, 3 * SC_TILE:]
            cu = c_gate * u
            back1 = jnp.where(row == 0, prev[7:8, :], pltpu.roll(cu, 1, axis=0))
            back2 = jnp.where(row == 0, prev[6:7, :],
                              jnp.where(row == 1, prev[7:8, :], pltpu.roll(cu, 2, axis=0)))
            y = cw[0:1, :] * back2 + cw[1:2, :] * back1 + cw[2:3, :] * cu + bias
            o_ref[r:r + sub, :] = (b_gate * y * _silu(gate)).astype(o_ref.dtype)
            prev = cu[sub - 8:, :]
        halo_ref[j] = prev

    @pl.when(j == 0)
    def _():
        run(True)

    @pl.when(j != 0)
    def _():
        run(False)


def norm_short_conv(x, g, w, conv_w, conv_b, seq, *, tm=1024):
    n, d = x.shape
    tm = min(tm, seq)
    n_tiles = SC_WIDTH // SC_TILE
    kernel = functools.partial(_norm_conv_kernel, sub=min(256, tm), blocks_per_seq=seq // tm)
    return pl.pallas_call(
        kernel,
        out_shape=jax.ShapeDtypeStruct((n, SC_WIDTH), BF16),
        grid=(n // tm, n_tiles),
        in_specs=[pl.BlockSpec((tm, d), lambda i, j: (i, 0)),
                  pl.BlockSpec((1, d), lambda i, j: (0, 0)),
                  pl.BlockSpec((d, 4 * SC_TILE), lambda i, j: (0, j)),
                  pl.BlockSpec((3, SC_TILE), lambda i, j: (0, j)),
                  pl.BlockSpec((1, SC_TILE), lambda i, j: (0, j))],
        out_specs=pl.BlockSpec((tm, SC_TILE), lambda i, j: (i, j)),
        scratch_shapes=[pltpu.VMEM((tm, d), BF16), pltpu.VMEM((n_tiles, 8, SC_TILE), F32)],
        compiler_params=_cparams(("arbitrary", "arbitrary")),
        name="norm_short_conv",
    )(x, g.reshape(1, d), w, conv_w, conv_b.reshape(1, SC_WIDTH))


def _rms(x, width):
    return lax.rsqrt(jnp.sum(x * x, axis=-1, keepdims=True) * (1.0 / width) + EPS)


def _mla_prep_kernel(c_ref, wq_ref, wk_ref, wvt_ref, qg_ref, kvg_ref, qn_ref, qr_ref, qrs_ref,
                     kn_ref, kr_ref, krs_ref, cos_ref, sin_ref,
                     q_out, k_out, vt_out, *, scale):
    c = c_ref[...].astype(F32)
    c_q = c[:, :MLA_Q_RANK]
    c_kv = c[:, MLA_Q_RANK:MLA_Q_RANK + MLA_KV_RANK]
    kro = c[:, MLA_Q_RANK + MLA_KV_RANK:MLA_Q_RANK + MLA_KV_RANK + LANES]
    kro_s = c[:, MLA_Q_RANK + MLA_KV_RANK + LANES:]
    cqn = (c_q * _rms(c_q, MLA_Q_RANK) * qg_ref[...]).astype(BF16)
    ckn = (c_kv * _rms(c_kv, MLA_KV_RANK) * kvg_ref[...]).astype(BF16)

    cos = cos_ref[...]
    sin = sin_ref[...]
    lane = lax.broadcasted_iota(jnp.int32, (1, LANES), 1)
    sin_signed = jnp.where(lane < MLA_ROPE // 2, -sin, sin)

    k_rot = (kro * kr_ref[...] * cos + kro_s * krs_ref[...] * sin_signed) * _rms(kro, MLA_ROPE)
    k_rot = k_rot.astype(k_out.dtype)

    vt = lax.dot_general(wvt_ref[...], ckn, (((1,), (1,)), ((), ())), preferred_element_type=F32)
    vt_out[0] = vt.reshape(vt_out.shape[1:]).astype(vt_out.dtype)

    for pair in range(MLA_HEADS // 2):
        q6 = jnp.dot(cqn, wq_ref[:, pair * 6 * LANES:(pair + 1) * 6 * LANES], preferred_element_type=F32)
        kn2 = jnp.dot(ckn, wk_ref[:, pair * 2 * LANES:(pair + 1) * 2 * LANES], preferred_element_type=F32)
        for sub in range(2):
            hh = 2 * pair + sub
            qn = q6[:, sub * 3 * LANES:sub * 3 * LANES + LANES]
            qr = q6[:, sub * 3 * LANES + LANES:sub * 3 * LANES + 2 * LANES]
            qs = q6[:, sub * 3 * LANES + 2 * LANES:(sub + 1) * 3 * LANES]
            qn = qn * _rms(qn, MLA_NOPE) * qn_ref[...]
            q_rot = (qr * qr_ref[...] * cos + qs * qrs_ref[...] * sin_signed) * _rms(qr, MLA_ROPE)
            q_out[hh, :, :LANES] = (qn * scale).astype(q_out.dtype)
            q_out[hh, :, LANES:] = (q_rot * scale).astype(q_out.dtype)

            kn = kn2[:, sub * LANES:(sub + 1) * LANES]
            kn = kn * _rms(kn, MLA_NOPE) * kn_ref[...]
            k_out[hh, :, :LANES] = kn.astype(k_out.dtype)
            k_out[hh, :, LANES:] = k_rot


def mla_prep(h, wq, wk, wvt, gains, cos, sin, *, tm):
    n = h.shape[0]
    cw = 1024
    row = lambda width, j=0: pl.BlockSpec((tm, width), lambda i: (i, j))
    full = lambda a: pl.BlockSpec(a.shape, lambda i: (0, 0))
    head_major = pl.BlockSpec((MLA_HEADS, tm, MLA_QK_PAD), lambda i: (0, i, 0))
    kernel = functools.partial(_mla_prep_kernel, scale=math.log2(math.e) / math.sqrt(MLA_NOPE + MLA_ROPE))
    return pl.pallas_call(
        kernel,
        out_shape=[jax.ShapeDtypeStruct((MLA_HEADS, n, MLA_QK_PAD), BF16),
                   jax.ShapeDtypeStruct((MLA_HEADS, n, MLA_QK_PAD), BF16),
                   jax.ShapeDtypeStruct((n // tm, MLA_HEADS, MLA_V, tm), BF16)],
        grid=(n // tm,),
        in_specs=[row(cw, 2), full(wq), full(wk), full(wvt)] + [full(a) for a in gains]
                 + [row(LANES), row(LANES)],
        out_specs=[head_major, head_major,
                   pl.BlockSpec((1, MLA_HEADS, MLA_V, tm), lambda i: (i, 0, 0, 0))],
        compiler_params=_cparams(("arbitrary",)),
        name="mla_prep",
    )(h, wq, wk, wvt, *gains, cos, sin)


def _mla_attn_kernel(q_ref, k_ref, vt_ref, g_ref, o_ref, s0_ref, s1_ref, m_ref, l_ref, acc_ref, *, tq, tk):
    qi = pl.program_id(2)
    m_ref[...] = jnp.full_like(m_ref, -jnp.inf)
    l_ref[...] = jnp.zeros_like(l_ref)
    acc_ref[...] = jnp.zeros_like(acc_ref)

    def scores(kb, dst, lo=0):
        ks = pl.multiple_of(kb * tk, tk)
        dst[:, lo:] = lax.dot_general(k_ref[pl.ds(ks, tk), :], q_ref[lo:, :], (((1,), (1,)), ((), ())),
                                      preferred_element_type=F32)

    def consume(src, kb, lo=0, diag=None):
        s = src[:, lo:]
        if diag is not None:
            key = diag + lax.broadcasted_iota(jnp.int32, s.shape, 0)
            qry = lo + lax.broadcasted_iota(jnp.int32, s.shape, 1)
            s = jnp.where(key // CHUNK <= qry // CHUNK, s, -jnp.inf)
        m_old = m_ref[:, lo:]
        m_new = jnp.maximum(m_old, jnp.max(s, axis=0, keepdims=True))
        alpha = jnp.exp2(m_old - m_new)
        p = jnp.exp2(s - m_new)
        l_ref[:, lo:] = alpha * l_ref[:, lo:] + jnp.sum(p, axis=0, keepdims=True)
        acc_ref[:, lo:] = alpha * acc_ref[:, lo:] + jnp.dot(vt_ref[kb, 0], p.astype(BF16),
                                                            preferred_element_type=F32)
        m_ref[:, lo:] = m_new

    def body(j, carry):
        scores(2 * j + 1, s1_ref)
        consume(s0_ref, 2 * j)
        scores(2 * j + 2, s0_ref)
        consume(s1_ref, 2 * j + 1)
        return carry

    scores(0, s0_ref)
    lax.fori_loop(0, qi, body, 0)
    scores(2 * qi + 1, s1_ref, lo=tk)
    consume(s0_ref, 2 * qi, diag=0)
    consume(s1_ref, 2 * qi + 1, lo=tk, diag=tk)
    o = (acc_ref[...] * (1.0 / l_ref[...])).T
    o_ref[...] = (o * _silu(g_ref[...].astype(F32))).astype(o_ref.dtype)


def mla_attention(qf, kf, vt, h, batch, seq, *, tq, tk):
    n = qf.shape[1]
    nq = seq // tq
    nkb = seq // tk
    kernel = functools.partial(_mla_attn_kernel, tq=tq, tk=tk)
    return pl.pallas_call(
        kernel,
        out_shape=jax.ShapeDtypeStruct((n, MLA_HEADS * MLA_V), BF16),
        grid=(batch, MLA_HEADS, nq),
        in_specs=[pl.BlockSpec((None, tq, MLA_QK_PAD), lambda b, hh, i: (hh, b * nq + i, 0)),
                  pl.BlockSpec((None, seq, MLA_QK_PAD), lambda b, hh, i: (hh, b, 0)),
                  pl.BlockSpec((nkb, 1, MLA_V, tk), lambda b, hh, i: (b, hh, 0, 0)),
                  pl.BlockSpec((tq, MLA_V), lambda b, hh, i: (b * nq + i, hh))],
        out_specs=pl.BlockSpec((tq, MLA_V), lambda b, hh, i: (b * nq + i, hh)),
        scratch_shapes=[pltpu.VMEM((tk, tq), F32), pltpu.VMEM((tk, tq), F32),
                        pltpu.VMEM((1, tq), F32), pltpu.VMEM((1, tq), F32), pltpu.VMEM((MLA_V, tq), F32)],
        compiler_params=_cparams(("arbitrary", "arbitrary", "arbitrary")),
        name="mla_attention",
    )(qf, kf, vt, h)


def _ret_kernel(q_ref, k_ref, v_ref, g_ref, cos_ref, sin_ref, on_ref, o_ref,
                state_ref, decay_ref, qd_ref, kd_ref, cd_ref, *, c):
    hh = pl.program_id(1)
    ci = pl.program_id(2)

    @pl.when(ci == 0)
    def _():
        state_ref[...] = jnp.zeros_like(state_ref)
        head = jnp.full((1, 1), hh, jnp.int32).astype(F32)
        log_gamma = jnp.log1p(-jnp.exp2(-5.0 - head))
        n_row = lax.broadcasted_iota(jnp.int32, (c, c), 0)
        n_col = lax.broadcasted_iota(jnp.int32, (c, c), 1)
        rel = (n_row - n_col).astype(F32)
        decay_ref[...] = jnp.where(rel >= 0, jnp.exp(log_gamma * jnp.maximum(rel, 0.0)), 0.0)
        n1 = lax.broadcasted_iota(jnp.int32, (c, 1), 0).astype(F32)
        qd_ref[...] = jnp.exp(log_gamma * (n1 + 1.0))
        kd_ref[...] = jnp.exp(log_gamma * (c - 1.0 - n1))
        cd_ref[...] = jnp.exp(log_gamma * c)

    cos = cos_ref[...]
    sin = sin_ref[...]
    half = RET_QK // 2

    def rope(x):
        x1, x2 = x[:, :half], x[:, half:]
        return jnp.concatenate([x1 * cos - x2 * sin, x1 * sin + x2 * cos], axis=1)

    q = rope(q_ref[...].astype(F32))
    k = rope(k_ref[...].astype(F32)) * (RET_QK ** -0.5)
    v = v_ref[...]
    state = state_ref[...]

    intra = lax.dot_general(q.astype(BF16), k.astype(BF16), (((1,), (1,)), ((), ())),
                            preferred_element_type=F32) * decay_ref[...]
    out = (jnp.dot(intra.astype(BF16), v, preferred_element_type=F32)
           + jnp.dot((q * qd_ref[...]).astype(BF16), state.astype(BF16), preferred_element_type=F32))
    kt = (k * kd_ref[...]).T.astype(BF16)
    state_ref[...] = state * cd_ref[...] + jnp.dot(kt, v, preferred_element_type=F32)

    normed = out * _rms(out, RET_V) * on_ref[...]
    o_ref[...] = (_silu(g_ref[...].astype(F32)) * normed).astype(o_ref.dtype)


def retention(h, cos, sin, out_norm, batch, seq, *, c=512):
    n = h.shape[0]
    c = min(c, seq)
    nc = seq // c
    nh = RET_HEADS
    kernel = functools.partial(_ret_kernel, c=c)
    return pl.pallas_call(
        kernel,
        out_shape=jax.ShapeDtypeStruct((n, nh * RET_V), BF16),
        grid=(batch, nh, nc),
        in_specs=[pl.BlockSpec((c, RET_QK), lambda b, hh, i: (b * nc + i, hh)),
                  pl.BlockSpec((c, RET_QK), lambda b, hh, i: (b * nc + i, nh + hh)),
                  pl.BlockSpec((c, RET_V), lambda b, hh, i: (b * nc + i, nh + hh)),
                  pl.BlockSpec((c, RET_V), lambda b, hh, i: (b * nc + i, 2 * nh + hh)),
                  pl.BlockSpec((c, LANES), lambda b, hh, i: (b * nc + i, 0)),
                  pl.BlockSpec((c, LANES), lambda b, hh, i: (b * nc + i, 0)),
                  pl.BlockSpec((1, RET_V), lambda b, hh, i: (0, hh))],
        out_specs=pl.BlockSpec((c, RET_V), lambda b, hh, i: (b * nc + i, hh)),
        scratch_shapes=[pltpu.VMEM((RET_QK, RET_V), F32), pltpu.VMEM((c, c), F32),
                        pltpu.VMEM((c, 1), F32), pltpu.VMEM((c, 1), F32), pltpu.VMEM((1, 1), F32)],
        compiler_params=_cparams(("arbitrary", "arbitrary", "arbitrary")),
        name="retention",
    )(h, h, h, h, cos, sin, out_norm.reshape(1, nh * RET_V))


def _rope_cols(w):
    half = MLA_ROPE // 2
    x1, x2 = w[..., :half], w[..., half:]
    z = jnp.zeros(w.shape[:-1] + (LANES - MLA_ROPE,), w.dtype)
    return jnp.concatenate([x1, x2, z, x2, x1, z], axis=-1)


def _mla_weights(w_in, w_uq, w_ukv):
    w_in, w_uq, w_ukv = w_in.astype(BF16), w_uq.astype(BF16), w_ukv.astype(BF16)
    q_end = MLA_Q_RANK
    kv_end = q_end + MLA_KV_RANK
    r_end = kv_end + MLA_ROPE
    w_in_x = jnp.concatenate([w_in[:, r_end:], w_in[:, :kv_end], _rope_cols(w_in[:, kv_end:r_end])], axis=1)
    uq = w_uq.reshape(MLA_Q_RANK, MLA_HEADS, MLA_NOPE + MLA_ROPE)
    wq = jnp.concatenate([uq[..., :MLA_NOPE], _rope_cols(uq[..., MLA_NOPE:])], axis=-1)
    wq = wq.reshape(MLA_Q_RANK, MLA_HEADS * 3 * LANES)
    ukv = w_ukv.reshape(MLA_KV_RANK, MLA_HEADS, MLA_NOPE + MLA_V)
    wk = ukv[..., :MLA_NOPE].reshape(MLA_KV_RANK, MLA_HEADS * MLA_NOPE)
    wvt = ukv[..., MLA_NOPE:].reshape(MLA_KV_RANK, MLA_HEADS * MLA_V).T
    return w_in_x, wq, wk, wvt


def _rope_gain(g):
    both = _rope_cols(g.reshape(1, MLA_ROPE))
    return both[:, :LANES], both[:, LANES:]


def kernel(x, positions, norm_g, sb_w_in, sb_w_out, sc_w_in, sc_conv_w, sc_conv_b, sc_w_out,
           mla_w_in, mla_q_norm, mla_w_uq, mla_kv_norm, mla_w_ukv, mla_qn_nope, mla_qn_rope,
           mla_kn_nope, mla_kn_rope, mla_w_out, ret_w_in, ret_out_norm, ret_w_out):
    batch, seq, d = x.shape
    n = batch * seq
    depth = norm_g.shape[0]
    xs = x.reshape(n, d)
    pos = positions.reshape(n, 1).astype(F32)
    mla_cos, mla_sin = rope_tables(pos, MLA_ROPE // 2)
    ret_cos, ret_sin = rope_tables(pos, RET_QK // 2)

    for i in range(depth):
        kind, j = i % 4, i // 4
        if kind == 0:
            q_scale = math.log2(math.e) / math.sqrt(SB_DIM)
            col = lax.broadcasted_iota(jnp.int32, (1, sb_w_in.shape[-1]), 1)
            w_in = sb_w_in[j] * jnp.where(col < SB_HEADS * SB_DIM, q_scale, 1.0)
            h = norm_matmul(xs, norm_g[i], w_in.astype(BF16))
            z = sb_attention(h, batch, seq)
            w_out = sb_w_out[j]
        elif kind == 1:
            w_in = sc_w_in[j].astype(BF16).reshape(d, 4, SC_WIDTH // SC_TILE, SC_TILE)
            w_in = w_in.transpose(0, 2, 1, 3).reshape(d, 4 * SC_WIDTH)
            z = norm_short_conv(xs, norm_g[i], w_in, sc_conv_w[j], sc_conv_b[j], seq)
            w_out = sc_w_out[j]
        elif kind == 2:
            w_in_x, wq, wk, wvt = _mla_weights(mla_w_in[j], mla_w_uq[j], mla_w_ukv[j])
            h = norm_matmul(xs, norm_g[i], w_in_x)
            qr, qrs = _rope_gain(mla_qn_rope[j])
            kr, krs = _rope_gain(mla_kn_rope[j])
            gains = [mla_q_norm[j].reshape(1, -1), mla_kv_norm[j].reshape(1, -1),
                     mla_qn_nope[j].reshape(1, -1), qr, qrs, mla_kn_nope[j].reshape(1, -1), kr, krs]
            tk = min(MLA_TK, seq // 2)
            qf, kf, vt = mla_prep(h, wq, wk, wvt, gains, mla_cos, mla_sin, tm=tk)
            z = mla_attention(qf, kf, vt, h, batch, seq, tq=2 * tk, tk=tk)
            w_out = mla_w_out[j]
        else:
            h = norm_matmul(xs, norm_g[i], ret_w_in[j].astype(BF16))
            z = retention(h, ret_cos, ret_sin, ret_out_norm[j], batch, seq)
            w_out = ret_w_out[j]
        xs = matmul_residual(z, w_out.astype(BF16), xs)
    return xs.reshape(batch, seq, d)
```

```python
import functools
import math

import jax
import jax.numpy as jnp
from jax import lax
from jax.experimental import pallas as pl
from jax.experimental.pallas import tpu as pltpu

F32 = jnp.float32
BF16 = jnp.bfloat16

D_MODEL = 2048
EPS = 1e-6
ROPE_BASE = 10000.0
CHUNK = 64

LANES = 128
SB_HEADS, SB_DIM = 16, 128
SC_WIDTH = 2048
MLA_HEADS, MLA_NOPE, MLA_ROPE, MLA_V = 16, 128, 64, 128
MLA_Q_RANK, MLA_KV_RANK = 512, 256
MLA_QK_PAD = 256
MLA_TK = 512
RET_HEADS, RET_QK, RET_V = 8, 256, 512

VMEM_LIMIT = 56 * 1024 * 1024

SB_T = 256
SB_GROUPS = 4
EXP2_ZERO_BELOW = -150.0


def _cparams(sem):
    return pltpu.CompilerParams(dimension_semantics=sem, vmem_limit_bytes=VMEM_LIMIT)


def _silu(x):
    return x * (1.0 / (1.0 + jnp.exp(-x)))


def _norm_matmul_kernel(x_ref, g_ref, w_ref, o_ref, xn_ref, *, sub):
    first_col = pl.program_id(1) == 0

    @pl.when(first_col)
    def _():
        for r in range(0, x_ref.shape[0], sub):
            x = x_ref[r:r + sub, :]
            ms = jnp.mean(x * x, axis=-1, keepdims=True)
            xn = (x * lax.rsqrt(ms + EPS) * g_ref[...]).astype(BF16)
            xn_ref[r:r + sub, :] = xn
            o_ref[r:r + sub, :] = jnp.dot(xn, w_ref[...], preferred_element_type=F32).astype(o_ref.dtype)

    @pl.when(jnp.logical_not(first_col))
    def _():
        o_ref[...] = jnp.dot(xn_ref[...], w_ref[...], preferred_element_type=F32).astype(o_ref.dtype)


def norm_matmul(x, g, w, *, tm=1024):
    n, d = x.shape
    dout = w.shape[1]
    tm = min(tm, n)
    tn = next(c for c in (2048, 1536, 1024, dout) if dout % c == 0)
    return pl.pallas_call(
        functools.partial(_norm_matmul_kernel, sub=min(256, tm)),
        out_shape=jax.ShapeDtypeStruct((n, dout), BF16),
        grid=(n // tm, dout // tn),
        in_specs=[pl.BlockSpec((tm, d), lambda i, j: (i, 0)),
                  pl.BlockSpec((1, d), lambda i, j: (0, 0)),
                  pl.BlockSpec((d, tn), lambda i, j: (0, j))],
        out_specs=pl.BlockSpec((tm, tn), lambda i, j: (i, j)),
        scratch_shapes=[pltpu.VMEM((tm, d), BF16)],
        compiler_params=_cparams(("arbitrary", "arbitrary")),
        name="norm_matmul",
    )(x, g.reshape(1, d), w)


def _matmul_residual_kernel(z_ref, w_ref, x_ref, o_ref):
    o_ref[...] = x_ref[...] + jnp.dot(z_ref[...], w_ref[...], preferred_element_type=F32)


def matmul_residual(z, w, x, *, tm=512):
    n, k = z.shape
    dout = w.shape[1]
    tm = min(tm, n)
    return pl.pallas_call(
        _matmul_residual_kernel,
        out_shape=jax.ShapeDtypeStruct((n, dout), F32),
        grid=(n // tm,),
        in_specs=[pl.BlockSpec((tm, k), lambda i: (i, 0)),
                  pl.BlockSpec((k, dout), lambda i: (0, 0), pipeline_mode=pl.Buffered(1)),
                  pl.BlockSpec((tm, dout), lambda i: (i, 0))],
        out_specs=pl.BlockSpec((tm, dout), lambda i: (i, 0)),
        compiler_params=_cparams(("arbitrary",)),
        name="matmul_residual",
    )(z, w, x)


def _rope_table_kernel(pos_ref, cos_ref, sin_ref, *, half):
    lane = lax.broadcasted_iota(jnp.int32, (1, LANES), 1) % half
    inv_freq = jnp.exp(lane.astype(F32) * (-math.log(ROPE_BASE) / half))
    ang = pos_ref[...] * inv_freq
    cos_ref[...] = jnp.cos(ang)
    sin_ref[...] = jnp.sin(ang)


def rope_tables(pos, half, *, tm=1024):
    n = pos.shape[0]
    tm = min(tm, n)
    spec = pl.BlockSpec((tm, LANES), lambda i: (i, 0))
    return pl.pallas_call(
        functools.partial(_rope_table_kernel, half=half),
        out_shape=[jax.ShapeDtypeStruct((n, LANES), F32)] * 2,
        grid=(n // tm,),
        in_specs=[pl.BlockSpec((tm, 1), lambda i: (i, 0))],
        out_specs=[spec, spec],
        compiler_params=_cparams(("arbitrary",)),
        name="rope_tables",
    )(pos)


def _sb_kernel(q_ref, k_ref, v_ref, g_ref, o_ref, vt_ref, acc_ref, r_ref, *, t, groups):
    qi = pl.program_id(2)

    @pl.when(qi == 0)
    def _():
        for j in range(vt_ref.shape[0]):
            vt_ref[j] = v_ref[j * t:(j + 1) * t, :].astype(F32).T.astype(BF16)

    acc_ref[...] = jnp.zeros_like(acc_ref)
    tq = groups * t
    key = lax.broadcasted_iota(jnp.int32, (t, t), 0)
    from_here = (lax.broadcasted_iota(jnp.int32, (t, t), 1) >= key).astype(BF16)
    lane = lax.broadcasted_iota(jnp.int32, (1, tq), 1)
    first = groups * qi

    group_lanes = [slice(g * t, (g + 1) * t) for g in range(groups)]
    strict = key < lax.broadcasted_iota(jnp.int32, (t, t), 1)

    def block_ids(d):
        return [jnp.maximum(first + g - d, 0) for g in range(groups)]

    def scan_keys(d, diag):
        zs = [lax.dot_general(k_ref[pl.ds(pl.multiple_of(kb * t, t), t), :], q_ref[lanes, :],
                              (((1,), (1,)), ((), ())), preferred_element_type=F32)
              for kb, lanes in zip(block_ids(d), group_lanes)]
        tails = []
        for z in zs:
            sp = jnp.maximum(z, 0.0) + jnp.log2(1.0 + jnp.exp2(-jnp.abs(z)))
            if diag:
                sp = jnp.where(strict, sp, 0.0)
            tails.append(jnp.dot(from_here, sp.astype(BF16), preferred_element_type=F32))
        return zs, tails

    def add_values(d, diag, r, scanned):
        zs, tails = scanned
        r = jnp.where(lane // t >= d - first, r, -jnp.inf)
        r_new = []
        for g, (kb, lanes) in enumerate(zip(block_ids(d), group_lanes)):
            a = jnp.exp2(zs[g] + r[:, lanes] - tails[g])
            if diag:
                a = jnp.where(strict, a, 0.0)
            acc_ref[:, lanes] += jnp.dot(vt_ref[kb], a.astype(BF16), preferred_element_type=F32)
            r_new.append(r[:, lanes] - tails[g][0:1, :])
        return jnp.concatenate(r_new, axis=1)

    scanned0 = scan_keys(0, True)
    scanned1 = scan_keys(1, False)
    r1 = add_values(0, True, jnp.zeros((1, tq), F32), scanned0)
    r2 = add_values(1, False, r1, scanned1)
    r_ref[...] = r2

    def cond(carry):
        d, rmax = carry
        return jnp.logical_and(d <= first + groups - 1, rmax >= EXP2_ZERO_BELOW)

    def body(carry):
        d, _ = carry
        r_new = add_values(d, False, r_ref[...], scan_keys(d, False))
        r_ref[...] = r_new
        return d + 1, jnp.max(r_new)

    lax.while_loop(cond, body, (jnp.int32(2), jnp.max(r2)))
    o_ref[...] = (acc_ref[...].T * _silu(g_ref[...].astype(F32))).astype(o_ref.dtype)


def sb_attention(h, batch, seq):
    n = h.shape[0]
    t = min(SB_T, seq)
    groups = min(SB_GROUPS, seq // t)
    tq = t * groups
    nq = seq // tq
    nh = SB_HEADS
    kernel = functools.partial(_sb_kernel, t=t, groups=groups)
    return pl.pallas_call(
        kernel,
        out_shape=jax.ShapeDtypeStruct((n, nh * SB_DIM), BF16),
        grid=(batch, nh, nq),
        in_specs=[pl.BlockSpec((tq, SB_DIM), lambda b, hh, i: (b * nq + i, hh)),
                  pl.BlockSpec((seq, SB_DIM), lambda b, hh, i: (b, nh + hh)),
                  pl.BlockSpec((seq, SB_DIM), lambda b, hh, i: (b, 2 * nh + hh)),
                  pl.BlockSpec((tq, SB_DIM), lambda b, hh, i: (b * nq + i, 3 * nh + hh))],
        out_specs=pl.BlockSpec((tq, SB_DIM), lambda b, hh, i: (b * nq + i, hh)),
        scratch_shapes=[pltpu.VMEM((seq // t, SB_DIM, t), BF16), pltpu.VMEM((SB_DIM, tq), F32),
                        pltpu.VMEM((1, tq), F32)],
        compiler_params=_cparams(("arbitrary", "arbitrary", "arbitrary")),
        name="sb_attention",
    )(h, h, h, h)


SC_TILE = 256


def _norm_conv_kernel(x_ref, g_ref, wb_ref, wc_ref, wu_ref, wg_ref, cw_ref, cb_ref, o_ref, xn_ref, halo_ref, *,
                      sub, blocks_per_seq):
    i = pl.program_id(0)
    j = pl.program_id(1)

    @pl.when(i % blocks_per_seq == 0)
    def _():
        halo_ref[j] = jnp.zeros(halo_ref.shape[1:], F32)

    def run(normalise):
        cw = cw_ref[...]
        bias = cb_ref[...]
        row = lax.broadcasted_iota(jnp.int32, (sub, 1), 0)
        prev = halo_ref[j]
        for r in range(0, x_ref.shape[0], sub):
            if normalise:
                x = x_ref[r:r + sub, :]
                ms = jnp.mean(x * x, axis=-1, keepdims=True)
                xn = (x * lax.rsqrt(ms + EPS) * g_ref[...]).astype(BF16)
                xn_ref[r:r + sub, :] = xn
            else:
                xn = xn_ref[r:r + sub, :]
            b_gate, c_gate, u, gate = (jnp.dot(xn, w[...], preferred_element_type=F32)
                                       for w in (wb_ref, wc_ref, wu_ref, wg_ref))
            cu = c_gate * u
            back1 = jnp.where(row == 0, prev[7:8, :], pltpu.roll(cu, 1, axis=0))
            back2 = jnp.where(row == 0, prev[6:7, :],
                              jnp.where(row == 1, prev[7:8, :], pltpu.roll(cu, 2, axis=0)))
            y = cw[0:1, :] * back2 + cw[1:2, :] * back1 + cw[2:3, :] * cu + bias
            o_ref[r:r + sub, :] = (b_gate * y * _silu(gate)).astype(o_ref.dtype)
            prev = cu[sub - 8:, :]
        halo_ref[j] = prev

    @pl.when(j == 0)
    def _():
        run(True)

    @pl.when(j != 0)
    def _():
        run(False)


def norm_short_conv(x, g, w, conv_w, conv_b, seq, *, tm=1024):
    n, d = x.shape
    tm = min(tm, seq)
    n_tiles = SC_WIDTH // SC_TILE
    kernel = functools.partial(_norm_conv_kernel, sub=min(256, tm), blocks_per_seq=seq // tm)
    return pl.pallas_call(
        kernel,
        out_shape=jax.ShapeDtypeStruct((n, SC_WIDTH), BF16),
        grid=(n // tm, n_tiles),
        in_specs=[pl.BlockSpec((tm, d), lambda i, j: (i, 0)),
                  pl.BlockSpec((1, d), lambda i, j: (0, 0)),
                  *[pl.BlockSpec((d, SC_TILE), functools.partial(lambda i, j, s: (0, s * n_tiles + j), s=s))
                    for s in range(4)],
                  pl.BlockSpec((3, SC_TILE), lambda i, j: (0, j)),
                  pl.BlockSpec((1, SC_TILE), lambda i, j: (0, j))],
        out_specs=pl.BlockSpec((tm, SC_TILE), lambda i, j: (i, j)),
        scratch_shapes=[pltpu.VMEM((tm, d), BF16), pltpu.VMEM((n_tiles, 8, SC_TILE), F32)],
        compiler_params=_cparams(("arbitrary", "arbitrary")),
        name="norm_short_conv",
    )(x, g.reshape(1, d), w, w, w, w, conv_w, conv_b.reshape(1, SC_WIDTH))


def _rms(x, width):
    return lax.rsqrt(jnp.sum(x * x, axis=-1, keepdims=True) * (1.0 / width) + EPS)


def _mla_prep_kernel(c_ref, wq_ref, wk_ref, wvt_ref, qg_ref, kvg_ref, qn_ref, qr_ref, qrs_ref,
                     kn_ref, kr_ref, krs_ref, cos_ref, sin_ref,
                     q_out, k_out, vt_out, *, scale):
    c = c_ref[...].astype(F32)
    c_q = c[:, :MLA_Q_RANK]
    c_kv = c[:, MLA_Q_RANK:MLA_Q_RANK + MLA_KV_RANK]
    kro = c[:, MLA_Q_RANK + MLA_KV_RANK:MLA_Q_RANK + MLA_KV_RANK + LANES]
    kro_s = c[:, MLA_Q_RANK + MLA_KV_RANK + LANES:]
    cqn = (c_q * _rms(c_q, MLA_Q_RANK) * qg_ref[...]).astype(BF16)
    ckn = (c_kv * _rms(c_kv, MLA_KV_RANK) * kvg_ref[...]).astype(BF16)

    cos = cos_ref[...]
    sin = sin_ref[...]
    lane = lax.broadcasted_iota(jnp.int32, (1, LANES), 1)
    sin_signed = jnp.where(lane < MLA_ROPE // 2, -sin, sin)

    k_rot = (kro * kr_ref[...] * cos + kro_s * krs_ref[...] * sin_signed) * _rms(kro, MLA_ROPE)
    k_rot = k_rot.astype(k_out.dtype)

    vt = lax.dot_general(wvt_ref[...], ckn, (((1,), (1,)), ((), ())), preferred_element_type=F32)
    vt_out[0] = vt.reshape(vt_out.shape[1:]).astype(vt_out.dtype)

    for pair in range(MLA_HEADS // 2):
        q6 = jnp.dot(cqn, wq_ref[:, pair * 6 * LANES:(pair + 1) * 6 * LANES], preferred_element_type=F32)
        kn2 = jnp.dot(ckn, wk_ref[:, pair * 2 * LANES:(pair + 1) * 2 * LANES], preferred_element_type=F32)
        for sub in range(2):
            hh = 2 * pair + sub
            qn = q6[:, sub * 3 * LANES:sub * 3 * LANES + LANES]
            qr = q6[:, sub * 3 * LANES + LANES:sub * 3 * LANES + 2 * LANES]
            qs = q6[:, sub * 3 * LANES + 2 * LANES:(sub + 1) * 3 * LANES]
            qn = qn * _rms(qn, MLA_NOPE) * qn_ref[...]
            q_rot = (qr * qr_ref[...] * cos + qs * qrs_ref[...] * sin_signed) * _rms(qr, MLA_ROPE)
            q_out[hh, :, :LANES] = (qn * scale).astype(q_out.dtype)
            q_out[hh, :, LANES:] = (q_rot * scale).astype(q_out.dtype)

            kn = kn2[:, sub * LANES:(sub + 1) * LANES]
            kn = kn * _rms(kn, MLA_NOPE) * kn_ref[...]
            k_out[hh, :, :LANES] = kn.astype(k_out.dtype)
            k_out[hh, :, LANES:] = k_rot


def mla_prep(h, wq, wk, wvt, gains, cos, sin, *, tm):
    n = h.shape[0]
    cw = 1024
    row = lambda width, j=0: pl.BlockSpec((tm, width), lambda i: (i, j))
    full = lambda a: pl.BlockSpec(a.shape, lambda i: (0, 0))
    head_major = pl.BlockSpec((MLA_HEADS, tm, MLA_QK_PAD), lambda i: (0, i, 0))
    kernel = functools.partial(_mla_prep_kernel, scale=math.log2(math.e) / math.sqrt(MLA_NOPE + MLA_ROPE))
    return pl.pallas_call(
        kernel,
        out_shape=[jax.ShapeDtypeStruct((MLA_HEADS, n, MLA_QK_PAD), BF16),
                   jax.ShapeDtypeStruct((MLA_HEADS, n, MLA_QK_PAD), BF16),
                   jax.ShapeDtypeStruct((n // tm, MLA_HEADS, MLA_V, tm), BF16)],
        grid=(n // tm,),
        in_specs=[row(cw, 2), full(wq), full(wk), full(wvt)] + [full(a) for a in gains]
                 + [row(LANES), row(LANES)],
        out_specs=[head_major, head_major,
                   pl.BlockSpec((1, MLA_HEADS, MLA_V, tm), lambda i: (i, 0, 0, 0))],
        compiler_params=_cparams(("arbitrary",)),
        name="mla_prep",
    )(h, wq, wk, wvt, *gains, cos, sin)


def _mla_attn_kernel(q_ref, k_ref, vt_ref, g_ref, o_ref, s0_ref, s1_ref, m_ref, l_ref, acc_ref, *, tq, tk):
    qi = pl.program_id(2)
    m_ref[...] = jnp.full_like(m_ref, -jnp.inf)
    l_ref[...] = jnp.zeros_like(l_ref)
    acc_ref[...] = jnp.zeros_like(acc_ref)

    def scores(kb, dst, lo=0):
        ks = pl.multiple_of(kb * tk, tk)
        dst[:, lo:] = lax.dot_general(k_ref[pl.ds(ks, tk), :], q_ref[lo:, :], (((1,), (1,)), ((), ())),
                                      preferred_element_type=F32)

    def consume(src, kb, lo=0, diag=None):
        s = src[:, lo:]
        if diag is not None:
            key = diag + lax.broadcasted_iota(jnp.int32, s.shape, 0)
            qry = lo + lax.broadcasted_iota(jnp.int32, s.shape, 1)
            s = jnp.where(key // CHUNK <= qry // CHUNK, s, -jnp.inf)
        m_old = m_ref[:, lo:]
        m_new = jnp.maximum(m_old, jnp.max(s, axis=0, keepdims=True))
        alpha = jnp.exp2(m_old - m_new)
        p = jnp.exp2(s - m_new)
        l_ref[:, lo:] = alpha * l_ref[:, lo:] + jnp.sum(p, axis=0, keepdims=True)
        acc_ref[:, lo:] = alpha * acc_ref[:, lo:] + jnp.dot(vt_ref[kb, 0], p.astype(BF16),
                                                            preferred_element_type=F32)
        m_ref[:, lo:] = m_new

    def body(j, carry):
        scores(2 * j + 1, s1_ref)
        consume(s0_ref, 2 * j)
        scores(2 * j + 2, s0_ref)
        consume(s1_ref, 2 * j + 1)
        return carry

    scores(0, s0_ref)
    lax.fori_loop(0, qi, body, 0)
    scores(2 * qi + 1, s1_ref, lo=tk)
    consume(s0_ref, 2 * qi, diag=0)
    consume(s1_ref, 2 * qi + 1, lo=tk, diag=tk)
    o = (acc_ref[...] * (1.0 / l_ref[...])).T
    o_ref[...] = (o * _silu(g_ref[...].astype(F32))).astype(o_ref.dtype)


def mla_attention(qf, kf, vt, h, batch, seq, *, tq, tk):
    n = qf.shape[1]
    nq = seq // tq
    nkb = seq // tk
    kernel = functools.partial(_mla_attn_kernel, tq=tq, tk=tk)
    return pl.pallas_call(
        kernel,
        out_shape=jax.ShapeDtypeStruct((n, MLA_HEADS * MLA_V), BF16),
        grid=(batch, MLA_HEADS, nq),
        in_specs=[pl.BlockSpec((None, tq, MLA_QK_PAD), lambda b, hh, i: (hh, b * nq + i, 0)),
                  pl.BlockSpec((None, seq, MLA_QK_PAD), lambda b, hh, i: (hh, b, 0)),
                  pl.BlockSpec((nkb, 1, MLA_V, tk), lambda b, hh, i: (b, hh, 0, 0)),
                  pl.BlockSpec((tq, MLA_V), lambda b, hh, i: (b * nq + i, hh))],
        out_specs=pl.BlockSpec((tq, MLA_V), lambda b, hh, i: (b * nq + i, hh)),
        scratch_shapes=[pltpu.VMEM((tk, tq), F32), pltpu.VMEM((tk, tq), F32),
                        pltpu.VMEM((1, tq), F32), pltpu.VMEM((1, tq), F32), pltpu.VMEM((MLA_V, tq), F32)],
        compiler_params=_cparams(("arbitrary", "arbitrary", "arbitrary")),
        name="mla_attention",
    )(qf, kf, vt, h)


def _ret_kernel(q_ref, k_ref, v_ref, g_ref, cos_ref, sin_ref, on_ref, o_ref,
                state_ref, decay_ref, qd_ref, kd_ref, cd_ref, *, c):
    hh = pl.program_id(1)
    ci = pl.program_id(2)

    @pl.when(ci == 0)
    def _():
        state_ref[...] = jnp.zeros_like(state_ref)
        head = jnp.full((1, 1), hh, jnp.int32).astype(F32)
        log_gamma = jnp.log1p(-jnp.exp2(-5.0 - head))
        n_row = lax.broadcasted_iota(jnp.int32, (c, c), 0)
        n_col = lax.broadcasted_iota(jnp.int32, (c, c), 1)
        rel = (n_row - n_col).astype(F32)
        decay_ref[...] = jnp.where(rel >= 0, jnp.exp(log_gamma * jnp.maximum(rel, 0.0)), 0.0)
        n1 = lax.broadcasted_iota(jnp.int32, (c, 1), 0).astype(F32)
        qd_ref[...] = jnp.exp(log_gamma * (n1 + 1.0))
        kd_ref[...] = jnp.exp(log_gamma * (c - 1.0 - n1))
        cd_ref[...] = jnp.exp(log_gamma * c)

    cos = cos_ref[...]
    sin = sin_ref[...]
    half = RET_QK // 2

    def rope(x):
        x1, x2 = x[:, :half], x[:, half:]
        return jnp.concatenate([x1 * cos - x2 * sin, x1 * sin + x2 * cos], axis=1)

    q = rope(q_ref[...].astype(F32))
    k = rope(k_ref[...].astype(F32)) * (RET_QK ** -0.5)
    v = v_ref[...]
    state = state_ref[...]

    intra = lax.dot_general(q.astype(BF16), k.astype(BF16), (((1,), (1,)), ((), ())),
                            preferred_element_type=F32) * decay_ref[...]
    out = (jnp.dot(intra.astype(BF16), v, preferred_element_type=F32)
           + jnp.dot((q * qd_ref[...]).astype(BF16), state.astype(BF16), preferred_element_type=F32))
    kt = (k * kd_ref[...]).T.astype(BF16)
    state_ref[...] = state * cd_ref[...] + jnp.dot(kt, v, preferred_element_type=F32)

    normed = out * _rms(out, RET_V) * on_ref[...]
    o_ref[...] = (_silu(g_ref[...].astype(F32)) * normed).astype(o_ref.dtype)


def retention(h, cos, sin, out_norm, batch, seq, *, c=512):
    n = h.shape[0]
    c = min(c, seq)
    nc = seq // c
    nh = RET_HEADS
    kernel = functools.partial(_ret_kernel, c=c)
    return pl.pallas_call(
        kernel,
        out_shape=jax.ShapeDtypeStruct((n, nh * RET_V), BF16),
        grid=(batch, nh, nc),
        in_specs=[pl.BlockSpec((c, RET_QK), lambda b, hh, i: (b * nc + i, hh)),
                  pl.BlockSpec((c, RET_QK), lambda b, hh, i: (b * nc + i, nh + hh)),
                  pl.BlockSpec((c, RET_V), lambda b, hh, i: (b * nc + i, nh + hh)),
                  pl.BlockSpec((c, RET_V), lambda b, hh, i: (b * nc + i, 2 * nh + hh)),
                  pl.BlockSpec((c, LANES), lambda b, hh, i: (b * nc + i, 0)),
                  pl.BlockSpec((c, LANES), lambda b, hh, i: (b * nc + i, 0)),
                  pl.BlockSpec((1, RET_V), lambda b, hh, i: (0, hh))],
        out_specs=pl.BlockSpec((c, RET_V), lambda b, hh, i: (b * nc + i, hh)),
        scratch_shapes=[pltpu.VMEM((RET_QK, RET_V), F32), pltpu.VMEM((c, c), F32),
                        pltpu.VMEM((c, 1), F32), pltpu.VMEM((c, 1), F32), pltpu.VMEM((1, 1), F32)],
        compiler_params=_cparams(("arbitrary", "arbitrary", "arbitrary")),
        name="retention",
    )(h, h, h, h, cos, sin, out_norm.reshape(1, nh * RET_V))


def _rope_cols(w):
    half = MLA_ROPE // 2
    x1, x2 = w[..., :half], w[..., half:]
    z = jnp.zeros(w.shape[:-1] + (LANES - MLA_ROPE,), w.dtype)
    return jnp.concatenate([x1, x2, z, x2, x1, z], axis=-1)


def _mla_weights(w_in, w_uq, w_ukv):
    w_in, w_uq, w_ukv = w_in.astype(BF16), w_uq.astype(BF16), w_ukv.astype(BF16)
    q_end = MLA_Q_RANK
    kv_end = q_end + MLA_KV_RANK
    r_end = kv_end + MLA_ROPE
    w_in_x = jnp.concatenate([w_in[:, r_end:], w_in[:, :kv_end], _rope_cols(w_in[:, kv_end:r_end])], axis=1)
    uq = w_uq.reshape(MLA_Q_RANK, MLA_HEADS, MLA_NOPE + MLA_ROPE)
    wq = jnp.concatenate([uq[..., :MLA_NOPE], _rope_cols(uq[..., MLA_NOPE:])], axis=-1)
    wq = wq.reshape(MLA_Q_RANK, MLA_HEADS * 3 * LANES)
    ukv = w_ukv.reshape(MLA_KV_RANK, MLA_HEADS, MLA_NOPE + MLA_V)
    wk = ukv[..., :MLA_NOPE].reshape(MLA_KV_RANK, MLA_HEADS * MLA_NOPE)
    wvt = ukv[..., MLA_NOPE:].reshape(MLA_KV_RANK, MLA_HEADS * MLA_V).T
    return w_in_x, wq, wk, wvt


def _rope_gain(g):
    both = _rope_cols(g.reshape(1, MLA_ROPE))
    return both[:, :LANES], both[:, LANES:]


def kernel(x, positions, norm_g, sb_w_in, sb_w_out, sc_w_in, sc_conv_w, sc_conv_b, sc_w_out,
           mla_w_in, mla_q_norm, mla_w_uq, mla_kv_norm, mla_w_ukv, mla_qn_nope, mla_qn_rope,
           mla_kn_nope, mla_kn_rope, mla_w_out, ret_w_in, ret_out_norm, ret_w_out):
    batch, seq, d = x.shape
    n = batch * seq
    depth = norm_g.shape[0]
    xs = x.reshape(n, d)
    pos = positions.reshape(n, 1).astype(F32)
    mla_cos, mla_sin = rope_tables(pos, MLA_ROPE // 2)
    ret_cos, ret_sin = rope_tables(pos, RET_QK // 2)

    for i in range(depth):
        kind, j = i % 4, i // 4
        if kind == 0:
            q_scale = math.log2(math.e) / math.sqrt(SB_DIM)
            col = lax.broadcasted_iota(jnp.int32, (1, sb_w_in.shape[-1]), 1)
            w_in = sb_w_in[j] * jnp.where(col < SB_HEADS * SB_DIM, q_scale, 1.0)
            h = norm_matmul(xs, norm_g[i], w_in.astype(BF16))
            z = sb_attention(h, batch, seq)
            w_out = sb_w_out[j]
        elif kind == 1:
            z = norm_short_conv(xs, norm_g[i], sc_w_in[j].astype(BF16), sc_conv_w[j], sc_conv_b[j], seq)
            w_out = sc_w_out[j]
        elif kind == 2:
            w_in_x, wq, wk, wvt = _mla_weights(mla_w_in[j], mla_w_uq[j], mla_w_ukv[j])
            h = norm_matmul(xs, norm_g[i], w_in_x)
            qr, qrs = _rope_gain(mla_qn_rope[j])
            kr, krs = _rope_gain(mla_kn_rope[j])
            gains = [mla_q_norm[j].reshape(1, -1), mla_kv_norm[j].reshape(1, -1),
                     mla_qn_nope[j].reshape(1, -1), qr, qrs, mla_kn_nope[j].reshape(1, -1), kr, krs]
            tk = min(MLA_TK, seq // 2)
            qf, kf, vt = mla_prep(h, wq, wk, wvt, gains, mla_cos, mla_sin, tm=tk)
            z = mla_attention(qf, kf, vt, h, batch, seq, tq=2 * tk, tk=tk)
            w_out = mla_w_out[j]
        else:
            h = norm_matmul(xs, norm_g[i], ret_w_in[j].astype(BF16))
            z = retention(h, ret_cos, ret_sin, ret_out_norm[j], batch, seq)
            w_out = ret_w_out[j]
        xs = matmul_residual(z, w_out.astype(BF16), xs)
    return xs.reshape(batch, seq, d)
```
